```python
import jax, jax.numpy as jnp
from jax import lax
import numpy as np

D_MODEL = 2048
BATCH = 1
SEQ = 16384
DEPTH = 4
DEC_BATCH = 4
DEC_SEQ = 2048
PAST_LEN = 128

N_HGRN_HEADS = 8
HGRN_DK = 128
HGRN_DV = 128
HGRN_WIDTH = N_HGRN_HEADS * HGRN_DK
HGRN_VWIDTH = N_HGRN_HEADS * HGRN_DV
CHUNK = 16
HEAD_DIM = 128
N_Q_HEADS = 8
N_KV_HEADS = 2
KV_GROUP = N_Q_HEADS // N_KV_HEADS
ATT_WIDTH = N_Q_HEADS * HEAD_DIM
KV_WIDTH = N_KV_HEADS * HEAD_DIM
Q_BLOCK = 128
GRID_W = 64
ROPE_THETA = 10000.0
ROPE_AXIS_DIM = HEAD_DIM // 2
MIX_WIDTH = HGRN_VWIDTH + ATT_WIDTH
IN_COLS = 3 * HGRN_WIDTH + 2 * HGRN_VWIDTH + ATT_WIDTH + 2 * KV_WIDTH
SPLIT_SIZES = (HGRN_WIDTH, HGRN_WIDTH, HGRN_WIDTH, HGRN_VWIDTH, HGRN_VWIDTH, ATT_WIDTH, KV_WIDTH, KV_WIDTH)
N_GROUPS = 4
EXPERTS_PER_GROUP = 8
N_EXPERTS = N_GROUPS * EXPERTS_PER_GROUP
TOP_K = 2
D_EXPERT = 1024
DISPATCH_BLOCK = 128
EPS = 1e-6

kernel_name = "hymba_hgrn2_axialgqa_hmoe_encoder"


def rms_norm(x, w):
    xf = x.astype(jnp.float32)
    y = xf * lax.rsqrt(jnp.mean(xf * xf, axis=-1, keepdims=True) + EPS)
    return (y * w.astype(jnp.float32)).astype(x.dtype)


def hgrn_lower_bounds(lb_param):
    p = jax.nn.softmax(lb_param.astype(jnp.float32), axis=1)
    return jnp.cumsum(p, axis=1) - p[:, :1]


def gla_chunkwise(q, k, v, logf):
    B, H, L, DK = q.shape
    DV = v.shape[-1]
    N = L // CHUNK
    q = q.reshape(B, H, N, CHUNK, DK)
    k = k.reshape(B, H, N, CHUNK, DK)
    v = v.reshape(B, H, N, CHUNK, DV)
    b = jnp.cumsum(logf.reshape(B, H, N, CHUNK, DK), axis=3)
    b_last = b[:, :, :, -1]
    causal_in_scan = jnp.tril(jnp.ones((CHUNK, CHUNK), dtype=bool))[:, :, None]
    diff = b[:, :, :, :, None, :] - b[:, :, :, None, :, :]
    decay = jnp.exp(jnp.where(causal_in_scan, diff, -jnp.inf))
    scores = jnp.einsum('bhnik,bhnjk,bhnijk->bhnij', q, k, decay)
    o_intra = jnp.einsum('bhnij,bhnjv->bhniv', scores, v)
    q_dec = q * jnp.exp(b)
    k_end = k * jnp.exp(b_last[:, :, :, None, :] - b)
    chunk_update = jnp.einsum('bhnjk,bhnjv->bhnkv', k_end, v)
    chunk_decay = jnp.exp(b_last)

    def step(S, inp):
        d, U = inp
        return d[..., None] * S + U, S

    S0 = jnp.zeros((B, H, DK, DV), jnp.float32)
    _, S_start = lax.scan(step, S0, (jnp.moveaxis(chunk_decay, 2, 0), jnp.moveaxis(chunk_update, 2, 0)))
    S_start = jnp.moveaxis(S_start, 0, 2)
    o_inter = jnp.einsum('bhnik,bhnkv->bhniv', q_dec, S_start)
    return (o_intra + o_inter).reshape(B, H, L, DV)


def hgrn2_bidirectional(q_raw, ff_raw, fb_raw, i_raw, g_raw, lb_fwd, lb_bwd, out_norm_w):
    B, L, _ = q_raw.shape

    def to_heads(t):
        return t.astype(jnp.float32).reshape(B, L, N_HGRN_HEADS, -1).transpose(0, 2, 1, 3)

    q = jax.nn.silu(to_heads(q_raw))
    v = to_heads(i_raw)

    def gates(f_raw, lb):
        lb = lb.reshape(N_HGRN_HEADS, 1, HGRN_DK)
        f = lb + (1.0 - lb) * jax.nn.sigmoid(to_heads(f_raw))
        return 1.0 - f, jnp.log(f)

    k_f, logf_f = gates(ff_raw, lb_fwd)
    o_fwd = gla_chunkwise(q, k_f, v, logf_f)
    k_b, logf_b = gates(fb_raw, lb_bwd)
    flip = lambda t: jnp.flip(t, axis=2)
    o_bwd = flip(gla_chunkwise(flip(q), flip(k_b), flip(v), flip(logf_b)))
    o = (o_fwd + o_bwd).transpose(0, 2, 1, 3)
    g = g_raw.astype(jnp.float32).reshape(B, L, N_HGRN_HEADS, HGRN_DV)
    o = rms_norm(o, out_norm_w) * jax.nn.silu(g)
    return o.reshape(B, L, HGRN_VWIDTH)


def axial_rope_tables(L):
    rows = L // GRID_W
    row = jnp.repeat(jnp.arange(rows, dtype=jnp.float32), GRID_W)
    col = jnp.tile(jnp.arange(GRID_W, dtype=jnp.float32), rows)
    inv_freq = ROPE_THETA ** (-jnp.arange(0, ROPE_AXIS_DIM, 2, dtype=jnp.float32) / ROPE_AXIS_DIM)
    ang = jnp.stack([row, col], axis=-1)[..., None] * inv_freq
    return jnp.cos(ang), jnp.sin(ang)


def apply_axial_rope(x, cos, sin):
    xs = x.reshape(*x.shape[:-1], 2, 2, ROPE_AXIS_DIM // 2)
    x1, x2 = xs[..., 0, :], xs[..., 1, :]
    c = cos[None, :, None]
    s = sin[None, :, None]
    out = jnp.stack([x1 * c - x2 * s, x2 * c + x1 * s], axis=-2)
    return out.reshape(x.shape)


def blocked_bidirectional_attention(q, k, v):
    B, L = q.shape[:2]
    nb = L // Q_BLOCK
    qb = q.reshape(B, nb, Q_BLOCK, N_KV_HEADS, KV_GROUP, HEAD_DIM).transpose(1, 0, 2, 3, 4, 5)
    scale = HEAD_DIM ** -0.5

    def one_block(qblk):
        s = jnp.einsum('bqhgd,bkhd->bhgqk', qblk, k) * scale
        p = jax.nn.softmax(s, axis=-1)
        return jnp.einsum('bhgqk,bkhd->bqhgd', p, v)

    o = lax.map(one_block, qb)
    return o.transpose(1, 0, 2, 3, 4, 5).reshape(B, L, ATT_WIDTH)


def token_mixer(h, w_in, lb_fwd, lb_bwd, hgrn_norm_w, q_norm_w, k_norm_w, attn_out_norm_w, w_out, cos, sin):
    B, L, _ = h.shape
    proj = h @ w_in
    q_h, f_fwd, f_bwd, i_h, g_h, q_a, k_a, v_a = jnp.split(proj, np.cumsum(SPLIT_SIZES)[:-1].tolist(), axis=-1)
    out_a = hgrn2_bidirectional(q_h, f_fwd, f_bwd, i_h, g_h, lb_fwd, lb_bwd, hgrn_norm_w)
    qa = rms_norm(q_a.astype(jnp.float32).reshape(B, L, N_Q_HEADS, HEAD_DIM), q_norm_w)
    ka = rms_norm(k_a.astype(jnp.float32).reshape(B, L, N_KV_HEADS, HEAD_DIM), k_norm_w)
    va = v_a.astype(jnp.float32).reshape(B, L, N_KV_HEADS, HEAD_DIM)
    qa = apply_axial_rope(qa, cos, sin).reshape(B, L, N_KV_HEADS, KV_GROUP, HEAD_DIM)
    ka = apply_axial_rope(ka, cos, sin)
    out_b = rms_norm(blocked_bidirectional_attention(qa, ka, va), attn_out_norm_w)
    merged = jnp.concatenate([out_a, out_b], axis=-1).astype(h.dtype)
    return merged @ w_out


def grouped_expert_ffn(xt, expert_id, weight, w_gate, w_up, w_down):
    T, D = xt.shape
    A = T * TOP_K
    eid = expert_id.reshape(A)
    tok = jnp.repeat(jnp.arange(T, dtype=jnp.int32), TOP_K)
    wt = weight.reshape(A)
    order = jnp.argsort(eid)
    eid_s, tok_s, wt_s = eid[order], tok[order], wt[order]
    counts = jnp.bincount(eid, length=N_EXPERTS)
    padded = (counts + DISPATCH_BLOCK - 1) // DISPATCH_BLOCK * DISPATCH_BLOCK
    seg_start = jnp.cumsum(counts) - counts
    pad_end = jnp.cumsum(padded)
    pad_start = pad_end - padded
    dest = pad_start[eid_s] + (jnp.arange(A, dtype=jnp.int32) - seg_start[eid_s])
    P = A + N_EXPERTS * DISPATCH_BLOCK
    row_tok = jnp.zeros((P,), jnp.int32).at[dest].set(tok_s)
    row_w = jnp.zeros((P,), jnp.float32).at[dest].set(wt_s)
    n_blocks = P // DISPATCH_BLOCK
    block_start = jnp.arange(n_blocks, dtype=jnp.int32) * DISPATCH_BLOCK
    block_expert = jnp.minimum(jnp.searchsorted(pad_end, block_start, side='right'), N_EXPERTS - 1)
    xrows = xt[row_tok].reshape(n_blocks, DISPATCH_BLOCK, D)

    def expert_block(args):
        xb, e = args
        hid = jax.nn.silu(xb @ w_gate[e]) * (xb @ w_up[e])
        return hid @ w_down[e]

    yrows = lax.map(expert_block, (xrows, block_expert)).reshape(P, D)
    out = jnp.zeros((T, D), jnp.float32).at[row_tok].add(yrows.astype(jnp.float32) * row_w[:, None])
    return out.astype(xt.dtype)


def hierarchical_moe(h, w_group, b_group, w_router, b_router, w_gate, w_up, w_down):
    B, L, D = h.shape
    T = B * L
    xt = h.reshape(T, D)
    g_logits = (xt @ w_group).astype(jnp.float32) + b_group.astype(jnp.float32)
    g_prob = jax.nn.softmax(g_logits, axis=-1)
    g_sel = jnp.argmax(g_logits, axis=-1)
    g_w = jnp.take_along_axis(g_prob, g_sel[:, None], axis=-1)
    e_logits = ((xt @ w_router).astype(jnp.float32) + b_router.astype(jnp.float32)).reshape(T, N_GROUPS, EXPERTS_PER_GROUP)
    e_in_group = jnp.take_along_axis(e_logits, g_sel[:, None, None], axis=1)[:, 0]
    top_vals, top_idx = lax.top_k(e_in_group, TOP_K)
    e_w = jax.nn.softmax(top_vals, axis=-1) * g_w
    expert_id = g_sel[:, None].astype(jnp.int32) * EXPERTS_PER_GROUP + top_idx.astype(jnp.int32)
    return grouped_expert_ffn(xt, expert_id, e_w, w_gate, w_up, w_down).reshape(B, L, D)


def run_trunk(x, mix_norm_w, w_in, lb, hgrn_out_norm_w, q_norm_w, k_norm_w, attn_out_norm_w, w_out,
              ffn_norm_w, w_group, b_group, w_router, b_router, w_gate_e, w_up_e, w_down_e, final_norm_w):
    cos, sin = axial_rope_tables(x.shape[1])
    h = x
    for l in range(DEPTH):
        h = h + token_mixer(rms_norm(h, mix_norm_w[l]), w_in[l], lb[0, l], lb[1, l], hgrn_out_norm_w[l],
                            q_norm_w[l], k_norm_w[l], attn_out_norm_w[l], w_out[l], cos, sin)
        h = h + hierarchical_moe(rms_norm(h, ffn_norm_w[l]), w_group[l], b_group[l], w_router[l], b_router[l],
                                 w_gate_e[l], w_up_e[l], w_down_e[l])
    return rms_norm(h, final_norm_w)


def setup_inputs(seed: int = 0) -> dict:
    key = jax.random.key(seed)
    ks = jax.random.split(key, 20)

    def nrm(k, shape, scale):
        return jax.random.normal(k, shape, jnp.float32) * scale

    return {
        "x_prompt": nrm(ks[0], (BATCH, SEQ, D_MODEL), 1.0),
        "x_sample": nrm(ks[1], (DEC_BATCH, DEC_SEQ, D_MODEL), 1.0),
        "mix_norm_w": 1.0 + nrm(ks[2], (DEPTH, D_MODEL), 0.02),
        "w_in": nrm(ks[3], (DEPTH, D_MODEL, IN_COLS), D_MODEL ** -0.5),
        "hgrn_lb": 1.0 + nrm(ks[4], (2, DEPTH, HGRN_WIDTH), 0.1),
        "hgrn_out_norm_w": 1.0 + nrm(ks[5], (DEPTH, HGRN_DV), 0.02),
        "q_norm_w": 1.0 + nrm(ks[6], (DEPTH, HEAD_DIM), 0.02),
        "k_norm_w": 1.0 + nrm(ks[7], (DEPTH, HEAD_DIM), 0.02),
        "attn_out_norm_w": 1.0 + nrm(ks[8], (DEPTH, ATT_WIDTH), 0.02),
        "w_out": nrm(ks[9], (DEPTH, MIX_WIDTH, D_MODEL), MIX_WIDTH ** -0.5),
        "ffn_norm_w": 1.0 + nrm(ks[10], (DEPTH, D_MODEL), 0.02),
        "w_group": nrm(ks[11], (DEPTH, D_MODEL, N_GROUPS), D_MODEL ** -0.5),
        "b_group": nrm(ks[12], (DEPTH, N_GROUPS), 0.01),
        "w_router": nrm(ks[13], (DEPTH, D_MODEL, N_EXPERTS), D_MODEL ** -0.5),
        "b_router": nrm(ks[14], (DEPTH, N_EXPERTS), 0.01),
        "w_gate_e": nrm(ks[15], (DEPTH, N_EXPERTS, D_MODEL, D_EXPERT), D_MODEL ** -0.5),
        "w_up_e": nrm(ks[16], (DEPTH, N_EXPERTS, D_MODEL, D_EXPERT), D_MODEL ** -0.5),
        "w_down_e": nrm(ks[17], (DEPTH, N_EXPERTS, D_EXPERT, D_MODEL), D_EXPERT ** -0.5),
        "final_norm_w": 1.0 + nrm(ks[18], (D_MODEL,), 0.02),
    }


def reference(x_prompt, x_sample, mix_norm_w, w_in, hgrn_lb, hgrn_out_norm_w, q_norm_w, k_norm_w,
              attn_out_norm_w, w_out, ffn_norm_w, w_group, b_group, w_router, b_router,
              w_gate_e, w_up_e, w_down_e, final_norm_w):
    lb = hgrn_lower_bounds(hgrn_lb)
    y_prompt = run_trunk(x_prompt, mix_norm_w, w_in, lb, hgrn_out_norm_w, q_norm_w, k_norm_w, attn_out_norm_w,
                         w_out, ffn_norm_w, w_group, b_group, w_router, b_router, w_gate_e, w_up_e, w_down_e,
                         final_norm_w)
    y_sample = run_trunk(x_sample, mix_norm_w, w_in, lb, hgrn_out_norm_w, q_norm_w, k_norm_w, attn_out_norm_w,
                         w_out, ffn_norm_w, w_group, b_group, w_router, b_router, w_gate_e, w_up_e, w_down_e,
                         final_norm_w)
    return (y_prompt, y_sample)
```

```python
import functools

import numpy as np
import jax
import jax.numpy as jnp
from jax import lax
from jax.experimental import pallas as pl
from jax.experimental.pallas import tpu as pltpu

F32 = jnp.float32
BF16 = jnp.bfloat16

D_MODEL = 2048
DEPTH = 4
N_HGRN_HEADS = 8
HEAD = 128
HGRN_WIDTH = N_HGRN_HEADS * HEAD
N_Q_HEADS = 8
N_KV_HEADS = 2
KV_GROUP = N_Q_HEADS // N_KV_HEADS
ATT_WIDTH = N_Q_HEADS * HEAD
KV_WIDTH = N_KV_HEADS * HEAD
GRID_W = 64
ROPE_THETA = 10000.0
ROPE_AXIS_DIM = HEAD // 2
IN_COLS = 5 * HGRN_WIDTH + ATT_WIDTH + 2 * KV_WIDTH
COL_QH, COL_FF, COL_FB, COL_IH, COL_GH = (i * HGRN_WIDTH for i in range(5))
COL_QA = 5 * HGRN_WIDTH
COL_KA = COL_QA + ATT_WIDTH
COL_VA = COL_KA + KV_WIDTH
N_GROUPS = 4
EXPERTS_PER_GROUP = 8
N_EXPERTS = N_GROUPS * EXPERTS_PER_GROUP
TOP_K = 2
D_EXPERT = 1024
EPS = 1e-6

LANES = 128
SUBLANES = 8
CHUNK = 128
VMEM_LIMIT = 56 * 1024 * 1024
MOE_TILE = 256

_NT = (((1,), (1,)), ((), ()))


def _cparams(sem):
    return pltpu.CompilerParams(dimension_semantics=sem, vmem_limit_bytes=VMEM_LIMIT)


def _rms_rows(x, w):
    ms = jnp.mean(x * x, axis=-1, keepdims=True)
    return x * lax.rsqrt(ms + EPS) * w


def _silu(x):
    return x * jax.nn.sigmoid(x)


def _row_tile(t, pref):
    while t % pref:
        pref //= 2
    return pref


def _for_row_chunks(n_rows, rc, fn):
    def body(c, carry):
        fn(pl.ds(pl.multiple_of(c * rc, rc), rc))
        return carry
    lax.fori_loop(0, n_rows // rc, body, 0)


def _norm_inproj_body(x_ref, nw_ref, w_ref, o_ref, xn_ref, *, tm, rc):
    @pl.when(pl.program_id(1) == 0)
    def _():
        def one(r):
            xn_ref[r, :] = _rms_rows(x_ref[r, :], nw_ref[...]).astype(BF16)
        _for_row_chunks(tm, rc, one)

    o_ref[...] = jnp.dot(xn_ref[...], w_ref[...].astype(BF16), preferred_element_type=F32)


def _norm_inproj(x, norm_w, w_in, layer):
    t, d = x.shape
    n = w_in.shape[-1]
    tm = _row_tile(t, 1024)
    tn = 512
    return pl.pallas_call(
        functools.partial(_norm_inproj_body, tm=tm, rc=min(256, tm)),
        grid=(t // tm, n // tn),
        in_specs=[
            pl.BlockSpec((tm, d), lambda i, j: (i, 0)),
            pl.BlockSpec((None, 1, d), lambda i, j: (layer, 0, 0)),
            pl.BlockSpec((None, d, tn), lambda i, j: (layer, 0, j)),
        ],
        out_specs=pl.BlockSpec((tm, tn), lambda i, j: (i, j)),
        out_shape=jax.ShapeDtypeStruct((t, n), F32),
        scratch_shapes=[pltpu.VMEM((tm, d), BF16)],
        compiler_params=_cparams(("parallel", "arbitrary")),
        name="norm_inproj",
    )(x, norm_w.reshape(-1, 1, d), w_in)


def _qkprep_body(pos_ref, q_ref, k_ref, v_ref, c_ref, s_ref, qw_ref, kw_ref, qo_ref, ko_ref, vo_ref):
    del pos_ref
    cos = c_ref[...]
    sin = s_ref[...]
    lane = lax.broadcasted_iota(jnp.int32, cos.shape, 1)
    first_half = (lane & (ROPE_AXIS_DIM // 2)) == 0
    scale = HEAD ** -0.5

    def rope(x):
        partner = jnp.where(first_half, pltpu.roll(x, LANES - ROPE_AXIS_DIM // 2, 1),
                            pltpu.roll(x, ROPE_AXIS_DIM // 2, 1))
        return x * cos + partner * sin

    for h in range(N_Q_HEADS):
        sl = slice(h * HEAD, (h + 1) * HEAD)
        qo_ref[:, sl] = (rope(_rms_rows(q_ref[:, sl], qw_ref[...])) * scale).astype(BF16)
    for h in range(N_KV_HEADS):
        sl = slice(h * HEAD, (h + 1) * HEAD)
        ko_ref[:, sl] = rope(_rms_rows(k_ref[:, sl], kw_ref[...])).astype(BF16)
    vo_ref[...] = v_ref[...].astype(BF16)


def _rope_tables(max_len):
    pos = jnp.arange(max_len, dtype=jnp.int32)
    row = (pos // GRID_W).astype(F32)
    col = (pos % GRID_W).astype(F32)
    inv_freq = ROPE_THETA ** (-jnp.arange(0, ROPE_AXIS_DIM, 2, dtype=F32) / ROPE_AXIS_DIM)
    ang_r = row[:, None] * inv_freq
    ang_c = col[:, None] * inv_freq
    cos = jnp.concatenate([jnp.cos(ang_r), jnp.cos(ang_r), jnp.cos(ang_c), jnp.cos(ang_c)], axis=-1)
    sin = jnp.concatenate([-jnp.sin(ang_r), jnp.sin(ang_r), -jnp.sin(ang_c), jnp.sin(ang_c)], axis=-1)
    return cos, sin


def _qkprep(proj, cos, sin, q_norm_w, k_norm_w, layer, seq_lens):
    t = proj.shape[0]
    tp = min(256, min(seq_lens))
    pos_blk = np.concatenate([np.arange(l // tp) for l in seq_lens]).astype(np.int32)
    grid_spec = pltpu.PrefetchScalarGridSpec(
        num_scalar_prefetch=1,
        grid=(t // tp,),
        in_specs=[
            pl.BlockSpec((tp, ATT_WIDTH), lambda i, p: (i, COL_QA // ATT_WIDTH)),
            pl.BlockSpec((tp, KV_WIDTH), lambda i, p: (i, COL_KA // KV_WIDTH)),
            pl.BlockSpec((tp, KV_WIDTH), lambda i, p: (i, COL_VA // KV_WIDTH)),
            pl.BlockSpec((tp, HEAD), lambda i, p: (p[i], 0)),
            pl.BlockSpec((tp, HEAD), lambda i, p: (p[i], 0)),
            pl.BlockSpec((None, 1, HEAD), lambda i, p: (layer, 0, 0)),
            pl.BlockSpec((None, 1, HEAD), lambda i, p: (layer, 0, 0)),
        ],
        out_specs=[
            pl.BlockSpec((tp, ATT_WIDTH), lambda i, p: (i, 0)),
            pl.BlockSpec((tp, KV_WIDTH), lambda i, p: (i, 0)),
            pl.BlockSpec((tp, KV_WIDTH), lambda i, p: (i, 0)),
        ],
    )
    return pl.pallas_call(
        _qkprep_body,
        grid_spec=grid_spec,
        out_shape=[jax.ShapeDtypeStruct((t, ATT_WIDTH), BF16),
                   jax.ShapeDtypeStruct((t, KV_WIDTH), BF16),
                   jax.ShapeDtypeStruct((t, KV_WIDTH), BF16)],
        compiler_params=_cparams(("parallel",)),
        name="qkprep",
    )(jnp.asarray(pos_blk), proj, proj, proj, cos, sin,
      q_norm_w.reshape(-1, 1, HEAD), k_norm_w.reshape(-1, 1, HEAD))


def _attn_body(qt_ref, kt_ref, first_ref, last_ref, q_ref, k_ref, v_ref, o_ref, m_ref, l_ref, acc_ref, *, tq):
    del qt_ref, kt_ref
    it = pl.program_id(1)

    @pl.when(first_ref[it] == 1)
    def _():
        m_ref[...] = jnp.full_like(m_ref, -jnp.inf)
        l_ref[...] = jnp.zeros_like(l_ref)
        acc_ref[...] = jnp.zeros_like(acc_ref)

    q4 = jnp.concatenate([q_ref[:, h * HEAD:(h + 1) * HEAD] for h in range(KV_GROUP)], axis=0)
    s = lax.dot_general(q4, k_ref[...], _NT, preferred_element_type=F32)
    m_prev = m_ref[...]
    m_new = jnp.maximum(m_prev, jnp.max(s, axis=-1, keepdims=True))
    alpha = jnp.exp(m_prev - m_new)
    p = jnp.exp(s - m_new[:, 0:1])
    l_ref[...] = alpha * l_ref[...] + jnp.sum(p, axis=-1, keepdims=True)
    acc_ref[...] = alpha * acc_ref[...] + jnp.dot(p.astype(BF16), v_ref[...], preferred_element_type=F32)
    m_ref[...] = m_new

    @pl.when(last_ref[it] == 1)
    def _():
        o = acc_ref[...] / l_ref[...]
        for h in range(KV_GROUP):
            o_ref[:, h * HEAD:(h + 1) * HEAD] = o[h * tq:(h + 1) * tq]


def _attention(qh, kh, vh, seq_lens):
    t = qh.shape[0]
    tq = min(256, min(seq_lens))
    tk = min(1024, min(seq_lens))
    qt, kt, first, last = [], [], [], []
    start = 0
    for l in seq_lens:
        nk = l // tk
        for qb in range(l // tq):
            for kb in range(nk):
                qt.append(start // tq + qb)
                kt.append(start // tk + kb)
                first.append(int(kb == 0))
                last.append(int(kb == nk - 1))
        start += l
    tabs = [jnp.asarray(np.asarray(a, np.int32)) for a in (qt, kt, first, last)]
    qw = KV_GROUP * HEAD
    grid_spec = pltpu.PrefetchScalarGridSpec(
        num_scalar_prefetch=4,
        grid=(N_KV_HEADS, len(qt)),
        in_specs=[
            pl.BlockSpec((tq, qw), lambda g, i, qt_, kt_, f_, l_: (qt_[i], g)),
            pl.BlockSpec((tk, HEAD), lambda g, i, qt_, kt_, f_, l_: (kt_[i], g)),
            pl.BlockSpec((tk, HEAD), lambda g, i, qt_, kt_, f_, l_: (kt_[i], g)),
        ],
        out_specs=pl.BlockSpec((tq, qw), lambda g, i, qt_, kt_, f_, l_: (qt_[i], g)),
        scratch_shapes=[pltpu.VMEM((KV_GROUP * tq, LANES), F32)] * 3,
    )
    return pl.pallas_call(
        functools.partial(_attn_body, tq=tq),
        grid_spec=grid_spec,
        out_shape=jax.ShapeDtypeStruct((t, ATT_WIDTH), F32),
        compiler_params=_cparams(("parallel", "arbitrary")),
        name="attention",
    )(*tabs, qh, kh, vh)


_LEVEL_HALVES = (8, 16, 32, 64)


def _hgrn_constants(rev):
    i = np.arange(CHUNK)
    tcol = np.arange(CHUNK)[None, :]
    refs = [i]
    for hs in _LEVEL_HALVES:
        blk = (i // (2 * hs)) * (2 * hs)
        refs.append(blk + (hs if rev else hs - 1))
    ref = np.concatenate(refs)[:, None]
    cum = (tcol >= ref) if rev else (tcol <= ref)
    wsel = (np.arange(SUBLANES * CHUNK)[:, None] // CHUNK) == (np.arange(CHUNK)[None, :] % SUBLANES)
    ii, jj = i[:, None], i[None, :]
    lvl = np.full((CHUNK, CHUNK), 5, np.int32)
    for n, hs in reversed(list(enumerate(_LEVEL_HALVES, start=1))):
        lvl[(ii // (2 * hs)) == (jj // (2 * hs))] = n
    lvl[(ii // SUBLANES) == (jj // SUBLANES)] = 0
    lvl[(jj < ii) if rev else (jj > ii)] = 5
    return (jnp.asarray(cum.astype(np.float32), BF16), jnp.asarray(wsel.astype(np.float32), BF16),
            jnp.asarray(lvl))


def _hgrn_body(reset_ref, q_ref, f_ref, v_ref, lb_ref, cum_ref, wsel_ref, lvl_ref, o_ref, st_ref,
               *, rev, nchunk, nblk):
    it = pl.program_id(1)
    blk = nblk - 1 - it if rev else it

    @pl.when(reset_ref[blk] == 1)
    def _():
        st_ref[...] = jnp.zeros_like(st_ref)

    lbv = lb_ref[...]
    lvl = lvl_ref[...]
    groups = CHUNK // SUBLANES
    sub = lax.broadcasted_iota(jnp.int32, (groups, SUBLANES, HEAD), 1)
    edge = 0 if rev else CHUNK - 1

    def chunk(ci, carry):
        c = nchunk - 1 - ci if rev else ci
        r = pl.ds(pl.multiple_of(c * CHUNK, CHUNK), CHUNK)
        q = _silu(q_ref[r, :])
        f = lbv + (1.0 - lbv) * jax.nn.sigmoid(f_ref[r, :])
        k = 1.0 - f
        v = v_ref[r, :]
        lf = jnp.log(f)
        hi = lf.astype(BF16)
        r1 = lf - hi.astype(F32)
        mid = r1.astype(BF16)
        lo = (r1 - mid.astype(F32)).astype(BF16)
        cs = jnp.dot(cum_ref[...], jnp.concatenate([hi, mid, lo], axis=1), preferred_element_type=F32)
        cs = cs[:, :HEAD] + cs[:, HEAD:2 * HEAD] + cs[:, 2 * HEAD:]
        b = cs[:CHUNK]

        level_scores = []
        for n in range(1, len(_LEVEL_HALVES) + 1):
            a = jnp.exp(-jnp.abs(b - cs[n * CHUNK:(n + 1) * CHUNK]))
            level_scores.append(lax.dot_general((q * a).astype(BF16), (k * a).astype(BF16), _NT,
                                                preferred_element_type=F32))
        b3 = b.reshape(groups, SUBLANES, HEAD)
        q3 = q.reshape(groups, SUBLANES, HEAD)
        k3 = k.reshape(groups, SUBLANES, HEAD)
        slabs = []
        for j in range(SUBLANES):
            bj = jnp.broadcast_to(b3[:, j:j + 1, :], b3.shape)
            kj = jnp.broadcast_to(k3[:, j:j + 1, :], k3.shape)
            ok = (sub <= j) if rev else (sub >= j)
            e = jnp.exp(jnp.where(ok, b3 - bj, -jnp.inf))
            slabs.append((q3 * kj * e).reshape(CHUNK, HEAD).astype(BF16))
        scores = jnp.dot(jnp.concatenate(slabs, axis=1), wsel_ref[...], preferred_element_type=F32)
        scores = jnp.where(lvl == 0, scores, 0.0)
        for n, s in enumerate(level_scores, start=1):
            scores = jnp.where(lvl == n, s, scores)
        o_intra = jnp.dot(scores.astype(BF16), v.astype(BF16), preferred_element_type=F32)

        st = st_ref[...]
        b_edge = b[edge:edge + 1, :]
        o_inter = lax.dot_general((q * jnp.exp(b)).astype(BF16), st.astype(BF16), _NT,
                                  preferred_element_type=F32)
        k_end = (k * jnp.exp(b_edge - b)).astype(BF16)
        st_ref[...] = st * jnp.exp(b_edge) + jnp.dot(v.T.astype(BF16), k_end, preferred_element_type=F32)
        o_ref[r, :] = o_intra + o_inter
        return carry

    lax.fori_loop(0, nchunk, chunk, 0)


def _hgrn(proj, lb, f_col, seq_lens, rev):
    t = proj.shape[0]
    lblk = min(512, min(seq_lens))
    nblk = t // lblk
    starts = np.cumsum([0] + list(seq_lens))
    reset = np.zeros((nblk,), np.int32)
    for s, l in zip(starts[:-1], seq_lens):
        reset[(s + l) // lblk - 1 if rev else s // lblk] = 1
    cum, wsel, lvl = _hgrn_constants(rev)

    def blk(i):
        return nblk - 1 - i if rev else i

    def col(off):
        return lambda h, i, rs: (blk(i), off // HEAD + h)

    const = lambda h, i, rs: (0, 0)
    grid_spec = pltpu.PrefetchScalarGridSpec(
        num_scalar_prefetch=1,
        grid=(N_HGRN_HEADS, nblk),
        in_specs=[
            pl.BlockSpec((lblk, HEAD), col(COL_QH)),
            pl.BlockSpec((lblk, HEAD), col(f_col)),
            pl.BlockSpec((lblk, HEAD), col(COL_IH)),
            pl.BlockSpec((None, 1, HEAD), lambda h, i, rs: (h, 0, 0)),
            pl.BlockSpec(cum.shape, const),
            pl.BlockSpec(wsel.shape, const),
            pl.BlockSpec(lvl.shape, const),
        ],
        out_specs=pl.BlockSpec((lblk, HEAD), lambda h, i, rs: (blk(i), h)),
        scratch_shapes=[pltpu.VMEM((HEAD, HEAD), F32)],
    )
    return pl.pallas_call(
        functools.partial(_hgrn_body, rev=rev, nchunk=lblk // CHUNK, nblk=nblk),
        grid_spec=grid_spec,
        out_shape=jax.ShapeDtypeStruct((t, HGRN_WIDTH), F32),
        compiler_params=_cparams(("parallel", "arbitrary")),
        name="hgrn_bwd" if rev else "hgrn_fwd",
    )(jnp.asarray(reset), proj, proj, proj, lb, cum, wsel, lvl)


def _merge_outproj_body(of_ref, ob_ref, g_ref, a_ref, x_ref, hw_ref, aw_ref, w_ref, o_ref, mg_ref, *, tm, rc):
    @pl.when(pl.program_id(1) == 0)
    def _():
        def one(r):
            for h in range(N_HGRN_HEADS):
                sl = slice(h * HEAD, (h + 1) * HEAD)
                o = of_ref[r, sl] + ob_ref[r, sl]
                mg_ref[r, sl] = (_rms_rows(o, hw_ref[...]) * _silu(g_ref[r, sl])).astype(BF16)
            mg_ref[r, HGRN_WIDTH:] = _rms_rows(a_ref[r, :], aw_ref[...]).astype(BF16)
        _for_row_chunks(tm, rc, one)

    o_ref[...] = x_ref[...] + jnp.dot(mg_ref[...], w_ref[...].astype(BF16), preferred_element_type=F32)


def _merge_outproj(o_f, o_b, proj, attn, x, hgrn_norm_w, attn_norm_w, w_out, layer):
    t, d = x.shape
    tm = _row_tile(t, 1024)
    tn = 512
    return pl.pallas_call(
        functools.partial(_merge_outproj_body, tm=tm, rc=min(256, tm)),
        grid=(t // tm, d // tn),
        in_specs=[
            pl.BlockSpec((tm, HGRN_WIDTH), lambda i, j: (i, 0)),
            pl.BlockSpec((tm, HGRN_WIDTH), lambda i, j: (i, 0)),
            pl.BlockSpec((tm, HGRN_WIDTH), lambda i, j: (i, COL_GH // HGRN_WIDTH)),
            pl.BlockSpec((tm, ATT_WIDTH), lambda i, j: (i, 0)),
            pl.BlockSpec((tm, tn), lambda i, j: (i, j)),
            pl.BlockSpec((None, 1, HEAD), lambda i, j: (layer, 0, 0)),
            pl.BlockSpec((None, 1, ATT_WIDTH), lambda i, j: (layer, 0, 0)),
            pl.BlockSpec((None, d, tn), lambda i, j: (layer, 0, j)),
        ],
        out_specs=pl.BlockSpec((tm, tn), lambda i, j: (i, j)),
        out_shape=jax.ShapeDtypeStruct((t, d), F32),
        scratch_shapes=[pltpu.VMEM((tm, d), BF16)],
        compiler_params=_cparams(("parallel", "arbitrary")),
        name="merge_outproj",
    )(o_f, o_b, proj, attn, x, hgrn_norm_w.reshape(-1, 1, HEAD), attn_norm_w.reshape(-1, 1, ATT_WIDTH), w_out)


def _router_body(x_ref, nw_ref, w_ref, b_ref, o_ref):
    xn = _rms_rows(x_ref[...], nw_ref[...])
    w = w_ref[...]
    xh = xn.astype(BF16)
    xl = (xn - xh.astype(F32)).astype(BF16)
    wh = w.astype(BF16)
    wl = (w - wh.astype(F32)).astype(BF16)
    logits = (jnp.dot(xh, wh, preferred_element_type=F32) + jnp.dot(xl, wh, preferred_element_type=F32)
              + jnp.dot(xh, wl, preferred_element_type=F32)) + b_ref[...]
    lane = lax.broadcasted_iota(jnp.int32, logits.shape, 1).astype(F32)
    neg = -jnp.inf
    big = float(LANES)

    def first_argmax(vals):
        top = jnp.max(vals, axis=-1, keepdims=True)
        idx = jnp.min(jnp.where(vals == top, lane, big), axis=-1, keepdims=True)
        return top, idx

    gl = jnp.where(lane < N_GROUPS, logits, neg)
    gmax, gsel = first_argmax(gl)
    g_w = 1.0 / jnp.sum(jnp.exp(gl - gmax), axis=-1, keepdims=True)
    lo = N_GROUPS + EXPERTS_PER_GROUP * gsel
    el = jnp.where((lane >= lo) & (lane < lo + EXPERTS_PER_GROUP), logits, neg)
    t1, i1 = first_argmax(el)
    t2, i2 = first_argmax(jnp.where(lane == i1, neg, el))
    e = jnp.exp(t2 - t1)
    w1 = g_w / (1.0 + e)
    w2 = g_w * e / (1.0 + e)
    o_ref[...] = jnp.where(lane == 0, i1 - N_GROUPS,
                           jnp.where(lane == 1, i2 - N_GROUPS, jnp.where(lane == 2, w1, jnp.where(lane == 3, w2, 0.0))))


def _router(h, ffn_norm_w, w_group, b_group, w_router, b_router, layer):
    t, d = h.shape
    tm = _row_tile(t, 256)
    pad = LANES - N_GROUPS - N_EXPERTS
    w = jnp.concatenate([w_group[layer], w_router[layer], jnp.zeros((d, pad), F32)], axis=1)
    b = jnp.concatenate([b_group[layer], b_router[layer], jnp.zeros((pad,), F32)])[None, :]
    return pl.pallas_call(
        _router_body,
        grid=(t // tm,),
        in_specs=[
            pl.BlockSpec((tm, d), lambda i: (i, 0)),
            pl.BlockSpec((None, 1, d), lambda i: (layer, 0, 0)),
            pl.BlockSpec((d, LANES), lambda i: (0, 0)),
            pl.BlockSpec((1, LANES), lambda i: (0, 0)),
        ],
        out_specs=pl.BlockSpec((tm, LANES), lambda i: (i, 0)),
        out_shape=jax.ShapeDtypeStruct((t, LANES), F32),
        compiler_params=_cparams(("parallel",)),
        name="router",
    )(h, ffn_norm_w.reshape(-1, 1, d), w, b)


def _dispatch_tables(route, t):
    tm = MOE_TILE
    a = t * TOP_K
    n_tiles = a // tm + N_EXPERTS
    p = n_tiles * tm
    eid = route[:, :TOP_K].astype(jnp.int32).reshape(a)
    wts = route[:, TOP_K:2 * TOP_K].reshape(a)
    order = jnp.argsort(eid, stable=True).astype(jnp.int32)
    sorted_e = eid[order]
    seg_start = jnp.searchsorted(sorted_e, jnp.arange(N_EXPERTS + 1, dtype=jnp.int32), side="left").astype(jnp.int32)
    counts = seg_start[1:] - seg_start[:-1]
    padded = (counts + tm - 1) // tm * tm
    pad_end = jnp.cumsum(padded)
    pad_start = pad_end - padded
    n_valid = (pad_end[-1] // tm).astype(jnp.int32)
    tile_start = jnp.arange(n_tiles, dtype=jnp.int32) * tm
    tile_expert = jnp.minimum(jnp.searchsorted(pad_end, tile_start, side="right"), N_EXPERTS - 1).astype(jnp.int32)
    last_valid_expert = tile_expert[jnp.maximum(n_valid - 1, 0)]
    tile_expert = jnp.where(jnp.arange(n_tiles) < n_valid, tile_expert, last_valid_expert)
    pos = jnp.arange(p, dtype=jnp.int32)
    pe = jnp.repeat(tile_expert, tm)
    within = pos - pad_start[pe]
    valid = (within < counts[pe]) & (pos < pad_end[-1])
    src = order[jnp.clip(seg_start[pe] + within, 0, a - 1)]
    row_tok = jnp.where(valid, src // TOP_K, 0).reshape(n_tiles, 1, tm)
    row_w = jnp.where(valid, wts[src], 0.0).reshape(p, 1)
    row_dst = jnp.where(valid, (src % TOP_K) * t + src // TOP_K, -1).reshape(n_tiles, 1, tm)
    return tile_expert, n_valid.reshape(1), row_tok, row_w, row_dst


def _cast_rows(src_ref, dst_ref, rc):
    def one(r):
        dst_ref[r, :] = src_ref[r, :].astype(dst_ref.dtype)
    _for_row_chunks(src_ref.shape[0], rc, one)


def _expert_changed(te_ref, i):
    return (i == 0) | (te_ref[i] != te_ref[jnp.maximum(i - 1, 0)])


def _gateup_body(te_ref, nv_ref, rt_ref, rtn_ref, x_hbm, nw_ref, wg_ref, wu_ref, o_ref, xbuf, wgb, wub, sem, *, tm):
    i = pl.program_id(0)
    nv = nv_ref[0]
    slot = i % 2

    def row_copy(tok, sl, r):
        return pltpu.make_async_copy(x_hbm.at[pl.ds(tok, 1), :], xbuf.at[sl, pl.ds(r, 1), :], sem.at[sl])

    def issue(idx_ref, sl):
        def body(r, carry):
            row_copy(idx_ref[0, r], sl, r).start()
            return carry
        lax.fori_loop(0, tm, body, 0)

    @pl.when((i == 0) & (nv > 0))
    def _():
        issue(rt_ref, 0)

    @pl.when(i + 1 < nv)
    def _():
        issue(rtn_ref, 1 - slot)

    @pl.when(i < nv)
    def _():
        def wait_row(r, carry):
            row_copy(0, slot, r).wait()
            return carry
        lax.fori_loop(0, tm, wait_row, 0)

        @pl.when(_expert_changed(te_ref, i))
        def _():
            _cast_rows(wg_ref, wgb, 256)
            _cast_rows(wu_ref, wub, 256)

        xn = _rms_rows(xbuf[slot], nw_ref[...]).astype(BF16)
        g = jnp.dot(xn, wgb[...], preferred_element_type=F32)
        u = jnp.dot(xn, wub[...], preferred_element_type=F32)
        o_ref[...] = (_silu(g) * u).astype(BF16)

    @pl.when(i >= nv)
    def _():
        o_ref[...] = jnp.zeros_like(o_ref)


def _expert_gateup(h, ffn_norm_w, w_gate, w_up, tile_expert, n_valid, row_tok, layer):
    t, d = h.shape
    tm = MOE_TILE
    n_tiles = row_tok.shape[0]
    de = w_gate.shape[-1]
    smem_rows = lambda f: pl.BlockSpec((None, 1, tm), f, memory_space=pltpu.SMEM)
    wspec = pl.BlockSpec((None, None, d, de), lambda i, te, nv: (layer, te[i], 0, 0))
    grid_spec = pltpu.PrefetchScalarGridSpec(
        num_scalar_prefetch=2,
        grid=(n_tiles,),
        in_specs=[
            smem_rows(lambda i, te, nv: (i, 0, 0)),
            smem_rows(lambda i, te, nv: (jnp.minimum(i + 1, n_tiles - 1), 0, 0)),
            pl.BlockSpec(memory_space=pl.ANY),
            pl.BlockSpec((None, 1, d), lambda i, te, nv: (layer, 0, 0)),
            wspec,
            wspec,
        ],
        out_specs=pl.BlockSpec((tm, de), lambda i, te, nv: (i, 0)),
        scratch_shapes=[pltpu.VMEM((2, tm, d), F32), pltpu.VMEM((d, de), BF16), pltpu.VMEM((d, de), BF16),
                        pltpu.SemaphoreType.DMA((2,))],
    )
    return pl.pallas_call(
        functools.partial(_gateup_body, tm=tm),
        grid_spec=grid_spec,
        out_shape=jax.ShapeDtypeStruct((n_tiles * tm, de), BF16),
        compiler_params=_cparams(("arbitrary",)),
        name="expert_gateup",
    )(tile_expert, n_valid, row_tok, row_tok, h, ffn_norm_w.reshape(-1, 1, d), w_gate, w_up)


def _down_body(te_ref, nv_ref, rd_ref, rdp_ref, h_ref, rw_ref, wd_ref, y_hbm, ybuf, wdb, sem, *, tm):
    i = pl.program_id(0)
    nv = nv_ref[0]
    slot = i % 2

    def scatter(idx_ref, sl, wait):
        def body(r, carry):
            dst = idx_ref[0, r]

            @pl.when(dst >= 0)
            def _():
                cp = pltpu.make_async_copy(ybuf.at[sl, pl.ds(r, 1), :], y_hbm.at[pl.ds(jnp.maximum(dst, 0), 1), :],
                                           sem.at[sl])
                if wait:
                    cp.wait()
                else:
                    cp.start()
            return carry
        lax.fori_loop(0, tm, body, 0)

    @pl.when(i < nv)
    def _():
        @pl.when(_expert_changed(te_ref, i))
        def _():
            _cast_rows(wd_ref, wdb, 256)

        ybuf[slot] = jnp.dot(h_ref[...], wdb[...], preferred_element_type=F32) * rw_ref[...]

        @pl.when(i >= 1)
        def _():
            scatter(rdp_ref, 1 - slot, True)

        scatter(rd_ref, slot, False)

        @pl.when(i == nv - 1)
        def _():
            scatter(rd_ref, slot, True)


def _expert_down(hid, w_down, tile_expert, n_valid, row_dst, row_w, t, layer):
    tm = MOE_TILE
    n_tiles = row_dst.shape[0]
    de, d = w_down.shape[-2:]
    smem_rows = lambda f: pl.BlockSpec((None, 1, tm), f, memory_space=pltpu.SMEM)
    grid_spec = pltpu.PrefetchScalarGridSpec(
        num_scalar_prefetch=2,
        grid=(n_tiles,),
        in_specs=[
            smem_rows(lambda i, te, nv: (i, 0, 0)),
            smem_rows(lambda i, te, nv: (jnp.maximum(i - 1, 0), 0, 0)),
            pl.BlockSpec((tm, de), lambda i, te, nv: (i, 0)),
            pl.BlockSpec((tm, 1), lambda i, te, nv: (i, 0)),
            pl.BlockSpec((None, None, de, d), lambda i, te, nv: (layer, te[i], 0, 0)),
        ],
        out_specs=pl.BlockSpec(memory_space=pl.ANY),
        scratch_shapes=[pltpu.VMEM((2, tm, d), F32), pltpu.VMEM((de, d), BF16), pltpu.SemaphoreType.DMA((2,))],
    )
    return pl.pallas_call(
        functools.partial(_down_body, tm=tm),
        grid_spec=grid_spec,
        out_shape=jax.ShapeDtypeStruct((TOP_K * t, d), F32),
        compiler_params=_cparams(("arbitrary",)),
        name="expert_down",
    )(tile_expert, n_valid, row_dst, row_dst, hid, row_w, w_down)


def _combine_body(h_ref, y0_ref, y1_ref, nw_ref, o_ref, *, final):
    s = h_ref[...] + (y0_ref[...] + y1_ref[...])
    o_ref[...] = _rms_rows(s, nw_ref[...]) if final else s


def _combine(h, y, final_norm_w, final):
    t, d = h.shape
    tm = _row_tile(t, 256)
    nb = t // tm
    return pl.pallas_call(
        functools.partial(_combine_body, final=final),
        grid=(nb,),
        in_specs=[
            pl.BlockSpec((tm, d), lambda i: (i, 0)),
            pl.BlockSpec((tm, d), lambda i: (i, 0)),
            pl.BlockSpec((tm, d), lambda i: (i + nb, 0)),
            pl.BlockSpec((1, d), lambda i: (0, 0)),
        ],
        out_specs=pl.BlockSpec((tm, d), lambda i: (i, 0)),
        out_shape=jax.ShapeDtypeStruct((t, d), F32),
        compiler_params=_cparams(("parallel",)),
        name="combine_final" if final else "combine",
    )(h, y, y, final_norm_w.reshape(1, d))


def _hgrn_lower_bounds(lb_param):
    p = jax.nn.softmax(lb_param.astype(F32), axis=1)
    return jnp.cumsum(p, axis=1) - p[:, :1]


def _trunk(x, seq_lens, mix_norm_w, w_in, hgrn_lb, hgrn_out_norm_w, q_norm_w, k_norm_w, attn_out_norm_w, w_out,
           ffn_norm_w, w_group, b_group, w_router, b_router, w_gate_e, w_up_e, w_down_e, final_norm_w):
    t = x.shape[0]
    depth = w_in.shape[0]
    lb = _hgrn_lower_bounds(hgrn_lb).reshape(2, depth, N_HGRN_HEADS, 1, HEAD)
    cos, sin = _rope_tables(max(seq_lens))
    h = x
    for l in range(depth):
        proj = _norm_inproj(h, mix_norm_w, w_in, l)
        qh, kh, vh = _qkprep(proj, cos, sin, q_norm_w, k_norm_w, l, seq_lens)
        attn = _attention(qh, kh, vh, seq_lens)
        o_f = _hgrn(proj, lb[0, l], COL_FF, seq_lens, rev=False)
        o_b = _hgrn(proj, lb[1, l], COL_FB, seq_lens, rev=True)
        h = _merge_outproj(o_f, o_b, proj, attn, h, hgrn_out_norm_w, attn_out_norm_w, w_out, l)
        route = _router(h, ffn_norm_w, w_group, b_group, w_router, b_router, l)
        tile_expert, n_valid, row_tok, row_w, row_dst = _dispatch_tables(route, t)
        hid = _expert_gateup(h, ffn_norm_w, w_gate_e, w_up_e, tile_expert, n_valid, row_tok, l)
        y = _expert_down(hid, w_down_e, tile_expert, n_valid, row_dst, row_w, t, l)
        h = _combine(h, y, final_norm_w, final=(l == depth - 1))
    return h


def kernel(x_prompt, x_sample, mix_norm_w, w_in, hgrn_lb, hgrn_out_norm_w, q_norm_w, k_norm_w, attn_out_norm_w, w_out,
           ffn_norm_w, w_group, b_group, w_router, b_router, w_gate_e, w_up_e, w_down_e, final_norm_w):
    bp, lp, d = x_prompt.shape
    bs, ls, _ = x_sample.shape
    seq_lens = (lp,) * bp + (ls,) * bs
    x = jnp.concatenate([x_prompt.reshape(bp * lp, d), x_sample.reshape(bs * ls, d)], axis=0)
    y = _trunk(x, seq_lens, mix_norm_w, w_in, hgrn_lb, hgrn_out_norm_w, q_norm_w, k_norm_w, attn_out_norm_w, w_out,
               ffn_norm_w, w_group, b_group, w_router, b_router, w_gate_e, w_up_e, w_down_e, final_norm_w)
    return y[:bp * lp].reshape(bp, lp, d), y[bp * lp:].reshape(bs, ls, d)
```

```python
import functools

import numpy as np
import jax
import jax.numpy as jnp
from jax import lax
from jax.experimental import pallas as pl
from jax.experimental.pallas import tpu as pltpu

F32 = jnp.float32
BF16 = jnp.bfloat16

D_MODEL = 2048
DEPTH = 4
N_HGRN_HEADS = 8
HEAD = 128
HGRN_WIDTH = N_HGRN_HEADS * HEAD
N_Q_HEADS = 8
N_KV_HEADS = 2
KV_GROUP = N_Q_HEADS // N_KV_HEADS
ATT_WIDTH = N_Q_HEADS * HEAD
KV_WIDTH = N_KV_HEADS * HEAD
GRID_W = 64
ROPE_THETA = 10000.0
ROPE_AXIS_DIM = HEAD // 2
IN_COLS = 5 * HGRN_WIDTH + ATT_WIDTH + 2 * KV_WIDTH
COL_QH, COL_FF, COL_FB, COL_IH, COL_GH = (i * HGRN_WIDTH for i in range(5))
COL_QA = 5 * HGRN_WIDTH
COL_KA = COL_QA + ATT_WIDTH
COL_VA = COL_KA + KV_WIDTH
N_GROUPS = 4
EXPERTS_PER_GROUP = 8
N_EXPERTS = N_GROUPS * EXPERTS_PER_GROUP
TOP_K = 2
D_EXPERT = 1024
EPS = 1e-6

LANES = 128
SUBLANES = 8
CHUNK = 128
VMEM_LIMIT = 56 * 1024 * 1024
MOE_TILE = 256
ATTN_Q_BLOCK = 1024
ATTN_CHAIN_ROWS = 512
DMA_ISSUE_UNROLL = 8

_NT = (((1,), (1,)), ((), ()))


def _cparams(sem):
    return pltpu.CompilerParams(dimension_semantics=sem, vmem_limit_bytes=VMEM_LIMIT)


def _rms_rows(x, w):
    ms = jnp.mean(x * x, axis=-1, keepdims=True)
    return x * lax.rsqrt(ms + EPS) * w


def _silu(x):
    return x * jax.nn.sigmoid(x)


def _row_tile(t, pref):
    while t % pref:
        pref //= 2
    return pref


def _for_row_chunks(n_rows, rc, fn):
    def body(c, carry):
        fn(pl.ds(pl.multiple_of(c * rc, rc), rc))
        return carry
    lax.fori_loop(0, n_rows // rc, body, 0)


def _norm_inproj_body(x_ref, nw_ref, w_ref, o_ref, xn_ref, *, tm, rc):
    @pl.when(pl.program_id(1) == 0)
    def _():
        def one(r):
            xn_ref[r, :] = _rms_rows(x_ref[r, :], nw_ref[...]).astype(BF16)
        _for_row_chunks(tm, rc, one)

    o_ref[...] = jnp.dot(xn_ref[...], w_ref[...].astype(BF16), preferred_element_type=F32)


def _norm_inproj(x, norm_w, w_in, layer):
    t, d = x.shape
    n = w_in.shape[-1]
    tm = _row_tile(t, 1024)
    tn = 512
    return pl.pallas_call(
        functools.partial(_norm_inproj_body, tm=tm, rc=min(256, tm)),
        grid=(t // tm, n // tn),
        in_specs=[
            pl.BlockSpec((tm, d), lambda i, j: (i, 0)),
            pl.BlockSpec((None, 1, d), lambda i, j: (layer, 0, 0)),
            pl.BlockSpec((None, d, tn), lambda i, j: (layer, 0, j)),
        ],
        out_specs=pl.BlockSpec((tm, tn), lambda i, j: (i, j)),
        out_shape=jax.ShapeDtypeStruct((t, n), F32),
        scratch_shapes=[pltpu.VMEM((tm, d), BF16)],
        compiler_params=_cparams(("parallel", "arbitrary")),
        name="norm_inproj",
    )(x, norm_w.reshape(-1, 1, d), w_in)


def _qkprep_body(pos_ref, q_ref, k_ref, v_ref, c_ref, s_ref, qw_ref, kw_ref, qo_ref, ko_ref, vo_ref):
    del pos_ref
    cos = c_ref[...]
    sin = s_ref[...]
    lane = lax.broadcasted_iota(jnp.int32, cos.shape, 1)
    first_half = (lane & (ROPE_AXIS_DIM // 2)) == 0
    scale = float(HEAD ** -0.5 * np.log2(np.e))

    def rope(x):
        partner = jnp.where(first_half, pltpu.roll(x, LANES - ROPE_AXIS_DIM // 2, 1),
                            pltpu.roll(x, ROPE_AXIS_DIM // 2, 1))
        return x * cos + partner * sin

    for h in range(N_Q_HEADS):
        sl = slice(h * HEAD, (h + 1) * HEAD)
        qo_ref[:, sl] = (rope(_rms_rows(q_ref[:, sl], qw_ref[...])) * scale).astype(BF16)
    for h in range(N_KV_HEADS):
        sl = slice(h * HEAD, (h + 1) * HEAD)
        ko_ref[:, sl] = rope(_rms_rows(k_ref[:, sl], kw_ref[...])).astype(BF16)
    vo_ref[...] = v_ref[...].astype(BF16)


def _rope_tables(max_len):
    pos = jnp.arange(max_len, dtype=jnp.int32)
    row = (pos // GRID_W).astype(F32)
    col = (pos % GRID_W).astype(F32)
    inv_freq = ROPE_THETA ** (-jnp.arange(0, ROPE_AXIS_DIM, 2, dtype=F32) / ROPE_AXIS_DIM)
    ang_r = row[:, None] * inv_freq
    ang_c = col[:, None] * inv_freq
    cos = jnp.concatenate([jnp.cos(ang_r), jnp.cos(ang_r), jnp.cos(ang_c), jnp.cos(ang_c)], axis=-1)
    sin = jnp.concatenate([-jnp.sin(ang_r), jnp.sin(ang_r), -jnp.sin(ang_c), jnp.sin(ang_c)], axis=-1)
    return cos, sin


def _qkprep(proj, cos, sin, q_norm_w, k_norm_w, layer, seq_lens):
    t = proj.shape[0]
    tp = min(256, min(seq_lens))
    pos_blk = np.concatenate([np.arange(l // tp) for l in seq_lens]).astype(np.int32)
    grid_spec = pltpu.PrefetchScalarGridSpec(
        num_scalar_prefetch=1,
        grid=(t // tp,),
        in_specs=[
            pl.BlockSpec((tp, ATT_WIDTH), lambda i, p: (i, COL_QA // ATT_WIDTH)),
            pl.BlockSpec((tp, KV_WIDTH), lambda i, p: (i, COL_KA // KV_WIDTH)),
            pl.BlockSpec((tp, KV_WIDTH), lambda i, p: (i, COL_VA // KV_WIDTH)),
            pl.BlockSpec((tp, HEAD), lambda i, p: (p[i], 0)),
            pl.BlockSpec((tp, HEAD), lambda i, p: (p[i], 0)),
            pl.BlockSpec((None, 1, HEAD), lambda i, p: (layer, 0, 0)),
            pl.BlockSpec((None, 1, HEAD), lambda i, p: (layer, 0, 0)),
        ],
        out_specs=[
            pl.BlockSpec((tp, ATT_WIDTH), lambda i, p: (i, 0)),
            pl.BlockSpec((tp, KV_WIDTH), lambda i, p: (i, 0)),
            pl.BlockSpec((tp, KV_WIDTH), lambda i, p: (i, 0)),
        ],
    )
    return pl.pallas_call(
        _qkprep_body,
        grid_spec=grid_spec,
        out_shape=[jax.ShapeDtypeStruct((t, ATT_WIDTH), BF16),
                   jax.ShapeDtypeStruct((t, KV_WIDTH), BF16),
                   jax.ShapeDtypeStruct((t, KV_WIDTH), BF16)],
        compiler_params=_cparams(("parallel",)),
        name="qkprep",
    )(jnp.asarray(pos_blk), proj, proj, proj, cos, sin,
      q_norm_w.reshape(-1, 1, HEAD), k_norm_w.reshape(-1, 1, HEAD))


def _attn_body(qt_ref, kt_ref, first_ref, last_ref, q_ref, k_ref, v_ref, o_ref, m_ref, acc_ref, *, tq):
    del qt_ref, kt_ref
    it = pl.program_id(1)

    @pl.when(first_ref[it] == 1)
    def _():
        m_ref[...] = jnp.full_like(m_ref, -jnp.inf)
        acc_ref[...] = jnp.zeros_like(acc_ref)

    k = k_ref[...]
    lane = lax.broadcasted_iota(jnp.int32, k.shape, 1)
    v1 = jnp.concatenate([v_ref[...], jnp.where(lane == 0, 1.0, 0.0).astype(BF16)], axis=1)
    rc = min(tq, ATTN_CHAIN_ROWS)
    for h, rb in [(h, rb) for h in range(KV_GROUP) for rb in range(tq // rc)]:
        rows = slice(h * tq + rb * rc, h * tq + (rb + 1) * rc)
        q = q_ref[rb * rc:(rb + 1) * rc, h * HEAD:(h + 1) * HEAD]
        s = lax.dot_general(q, k, _NT, preferred_element_type=F32)
        m_prev = m_ref[rows, :]
        m_new = jnp.maximum(m_prev, jnp.max(s, axis=-1, keepdims=True))
        alpha = jnp.exp2(m_prev - m_new)
        p = jnp.exp2(s - m_new[:, 0:1]).astype(BF16)
        pv = jnp.dot(p, v1, preferred_element_type=F32)
        acc = acc_ref[rows, :]
        acc_ref[rows, :] = jnp.concatenate([alpha * acc[:, :HEAD], alpha * acc[:, HEAD:]], axis=1) + pv
        m_ref[rows, :] = m_new

    @pl.when(last_ref[it] == 1)
    def _():
        for h in range(KV_GROUP):
            acc = acc_ref[h * tq:(h + 1) * tq, :]
            o_ref[:, h * HEAD:(h + 1) * HEAD] = acc[:, :HEAD] / acc[:, HEAD:HEAD + 1]


def _attention(qh, kh, vh, seq_lens):
    t = qh.shape[0]
    tq = min(ATTN_Q_BLOCK, min(seq_lens))
    tk = min(1024, min(seq_lens))
    qt, kt, first, last = [], [], [], []
    start = 0
    for l in seq_lens:
        nk = l // tk
        for qb in range(l // tq):
            for kb in range(nk):
                qt.append(start // tq + qb)
                kt.append(start // tk + kb)
                first.append(int(kb == 0))
                last.append(int(kb == nk - 1))
        start += l
    tabs = [jnp.asarray(np.asarray(a, np.int32)) for a in (qt, kt, first, last)]
    qw = KV_GROUP * HEAD
    grid_spec = pltpu.PrefetchScalarGridSpec(
        num_scalar_prefetch=4,
        grid=(N_KV_HEADS, len(qt)),
        in_specs=[
            pl.BlockSpec((tq, qw), lambda g, i, qt_, kt_, f_, l_: (qt_[i], g)),
            pl.BlockSpec((tk, HEAD), lambda g, i, qt_, kt_, f_, l_: (kt_[i], g)),
            pl.BlockSpec((tk, HEAD), lambda g, i, qt_, kt_, f_, l_: (kt_[i], g)),
        ],
        out_specs=pl.BlockSpec((tq, qw), lambda g, i, qt_, kt_, f_, l_: (qt_[i], g)),
        scratch_shapes=[pltpu.VMEM((KV_GROUP * tq, LANES), F32), pltpu.VMEM((KV_GROUP * tq, 2 * HEAD), F32)],
    )
    return pl.pallas_call(
        functools.partial(_attn_body, tq=tq),
        grid_spec=grid_spec,
        out_shape=jax.ShapeDtypeStruct((t, ATT_WIDTH), F32),
        compiler_params=_cparams(("parallel", "arbitrary")),
        name="attention",
    )(*tabs, qh, kh, vh)


_LEVEL_HALVES = (8, 16, 32, 64)


def _hgrn_constants(rev):
    i = np.arange(CHUNK)
    tcol = np.arange(CHUNK)[None, :]
    refs = [i]
    for hs in _LEVEL_HALVES:
        blk = (i // (2 * hs)) * (2 * hs)
        refs.append(blk + (hs if rev else hs - 1))
    ref = np.concatenate(refs)[:, None]
    cum = (tcol >= ref) if rev else (tcol <= ref)
    wsel = (np.arange(SUBLANES * CHUNK)[:, None] // CHUNK) == (np.arange(CHUNK)[None, :] % SUBLANES)
    ii, jj = i[:, None], i[None, :]
    lvl = np.full((CHUNK, CHUNK), 5, np.int32)
    for n, hs in reversed(list(enumerate(_LEVEL_HALVES, start=1))):
        lvl[(ii // (2 * hs)) == (jj // (2 * hs))] = n
    lvl[(ii // SUBLANES) == (jj // SUBLANES)] = 0
    lvl[(jj < ii) if rev else (jj > ii)] = 5
    return (jnp.asarray(cum.astype(np.float32), BF16), jnp.asarray(wsel.astype(np.float32), BF16),
            jnp.asarray(lvl))


def _hgrn_body(reset_ref, q_ref, f_ref, v_ref, lb_ref, cum_ref, wsel_ref, lvl_ref, o_ref, st_ref,
               *, rev, nchunk, nblk):
    it = pl.program_id(1)
    blk = nblk - 1 - it if rev else it

    @pl.when(reset_ref[blk] == 1)
    def _():
        st_ref[...] = jnp.zeros_like(st_ref)

    lbv = lb_ref[...]
    lvl = lvl_ref[...]
    groups = CHUNK // SUBLANES
    sub = lax.broadcasted_iota(jnp.int32, (groups, SUBLANES, HEAD), 1)
    edge = 0 if rev else CHUNK - 1

    def chunk(ci, carry):
        c = nchunk - 1 - ci if rev else ci
        r = pl.ds(pl.multiple_of(c * CHUNK, CHUNK), CHUNK)
        q = _silu(q_ref[r, :])
        f = lbv + (1.0 - lbv) * jax.nn.sigmoid(f_ref[r, :])
        k = 1.0 - f
        v = v_ref[r, :]
        lf = jnp.log(f)
        hi = lf.astype(BF16)
        r1 = lf - hi.astype(F32)
        mid = r1.astype(BF16)
        lo = (r1 - mid.astype(F32)).astype(BF16)
        cs = jnp.dot(cum_ref[...], jnp.concatenate([hi, mid, lo], axis=1), preferred_element_type=F32)
        cs = cs[:, :HEAD] + cs[:, HEAD:2 * HEAD] + cs[:, 2 * HEAD:]
        b = cs[:CHUNK]

        level_scores = []
        for n in range(1, len(_LEVEL_HALVES) + 1):
            a = jnp.exp(-jnp.abs(b - cs[n * CHUNK:(n + 1) * CHUNK]))
            level_scores.append(lax.dot_general((q * a).astype(BF16), (k * a).astype(BF16), _NT,
                                                preferred_element_type=F32))
        b3 = b.reshape(groups, SUBLANES, HEAD)
        q3 = q.reshape(groups, SUBLANES, HEAD)
        k3 = k.reshape(groups, SUBLANES, HEAD)
        slabs = []
        for j in range(SUBLANES):
            bj = jnp.broadcast_to(b3[:, j:j + 1, :], b3.shape)
            kj = jnp.broadcast_to(k3[:, j:j + 1, :], k3.shape)
            ok = (sub <= j) if rev else (sub >= j)
            e = jnp.exp(jnp.where(ok, b3 - bj, -jnp.inf))
            slabs.append((q3 * kj * e).reshape(CHUNK, HEAD).astype(BF16))
        scores = jnp.dot(jnp.concatenate(slabs, axis=1), wsel_ref[...], preferred_element_type=F32)
        scores = jnp.where(lvl == 0, scores, 0.0)
        for n, s in enumerate(level_scores, start=1):
            scores = jnp.where(lvl == n, s, scores)
        o_intra = jnp.dot(scores.astype(BF16), v.astype(BF16), preferred_element_type=F32)

        st = st_ref[...]
        b_edge = b[edge:edge + 1, :]
        o_inter = lax.dot_general((q * jnp.exp(b)).astype(BF16), st.astype(BF16), _NT,
                                  preferred_element_type=F32)
        k_end = (k * jnp.exp(b_edge - b)).astype(BF16)
        st_ref[...] = st * jnp.exp(b_edge) + jnp.dot(v.T.astype(BF16), k_end, preferred_element_type=F32)
        o_ref[r, :] = o_intra + o_inter
        return carry

    lax.fori_loop(0, nchunk, chunk, 0)


def _hgrn(proj, lb, f_col, seq_lens, rev):
    t = proj.shape[0]
    lblk = min(512, min(seq_lens))
    nblk = t // lblk
    starts = np.cumsum([0] + list(seq_lens))
    reset = np.zeros((nblk,), np.int32)
    for s, l in zip(starts[:-1], seq_lens):
        reset[(s + l) // lblk - 1 if rev else s // lblk] = 1
    cum, wsel, lvl = _hgrn_constants(rev)

    def blk(i):
        return nblk - 1 - i if rev else i

    def col(off):
        return lambda h, i, rs: (blk(i), off // HEAD + h)

    const = lambda h, i, rs: (0, 0)
    grid_spec = pltpu.PrefetchScalarGridSpec(
        num_scalar_prefetch=1,
        grid=(N_HGRN_HEADS, nblk),
        in_specs=[
            pl.BlockSpec((lblk, HEAD), col(COL_QH)),
            pl.BlockSpec((lblk, HEAD), col(f_col)),
            pl.BlockSpec((lblk, HEAD), col(COL_IH)),
            pl.BlockSpec((None, 1, HEAD), lambda h, i, rs: (h, 0, 0)),
            pl.BlockSpec(cum.shape, const),
            pl.BlockSpec(wsel.shape, const),
            pl.BlockSpec(lvl.shape, const),
        ],
        out_specs=pl.BlockSpec((lblk, HEAD), lambda h, i, rs: (blk(i), h)),
        scratch_shapes=[pltpu.VMEM((HEAD, HEAD), F32)],
    )
    return pl.pallas_call(
        functools.partial(_hgrn_body, rev=rev, nchunk=lblk // CHUNK, nblk=nblk),
        grid_spec=grid_spec,
        out_shape=jax.ShapeDtypeStruct((t, HGRN_WIDTH), F32),
        compiler_params=_cparams(("parallel", "arbitrary")),
        name="hgrn_bwd" if rev else "hgrn_fwd",
    )(jnp.asarray(reset), proj, proj, proj, lb, cum, wsel, lvl)


def _merge_outproj_body(of_ref, ob_ref, g_ref, a_ref, x_ref, hw_ref, aw_ref, w_ref, o_ref, mg_ref, *, tm, rc):
    @pl.when(pl.program_id(1) == 0)
    def _():
        def one(r):
            for h in range(N_HGRN_HEADS):
                sl = slice(h * HEAD, (h + 1) * HEAD)
                o = of_ref[r, sl] + ob_ref[r, sl]
                mg_ref[r, sl] = (_rms_rows(o, hw_ref[...]) * _silu(g_ref[r, sl])).astype(BF16)
            mg_ref[r, HGRN_WIDTH:] = _rms_rows(a_ref[r, :], aw_ref[...]).astype(BF16)
        _for_row_chunks(tm, rc, one)

    o_ref[...] = x_ref[...] + jnp.dot(mg_ref[...], w_ref[...].astype(BF16), preferred_element_type=F32)


def _merge_outproj(o_f, o_b, proj, attn, x, hgrn_norm_w, attn_norm_w, w_out, layer):
    t, d = x.shape
    tm = _row_tile(t, 1024)
    tn = 512
    return pl.pallas_call(
        functools.partial(_merge_outproj_body, tm=tm, rc=min(256, tm)),
        grid=(t // tm, d // tn),
        in_specs=[
            pl.BlockSpec((tm, HGRN_WIDTH), lambda i, j: (i, 0)),
            pl.BlockSpec((tm, HGRN_WIDTH), lambda i, j: (i, 0)),
            pl.BlockSpec((tm, HGRN_WIDTH), lambda i, j: (i, COL_GH // HGRN_WIDTH)),
            pl.BlockSpec((tm, ATT_WIDTH), lambda i, j: (i, 0)),
            pl.BlockSpec((tm, tn), lambda i, j: (i, j)),
            pl.BlockSpec((None, 1, HEAD), lambda i, j: (layer, 0, 0)),
            pl.BlockSpec((None, 1, ATT_WIDTH), lambda i, j: (layer, 0, 0)),
            pl.BlockSpec((None, d, tn), lambda i, j: (layer, 0, j)),
        ],
        out_specs=pl.BlockSpec((tm, tn), lambda i, j: (i, j)),
        out_shape=jax.ShapeDtypeStruct((t, d), F32),
        scratch_shapes=[pltpu.VMEM((tm, d), BF16)],
        compiler_params=_cparams(("parallel", "arbitrary")),
        name="merge_outproj",
    )(o_f, o_b, proj, attn, x, hgrn_norm_w.reshape(-1, 1, HEAD), attn_norm_w.reshape(-1, 1, ATT_WIDTH), w_out)


def _router_body(x_ref, nw_ref, w_ref, b_ref, o_ref, cnt_ref):
    @pl.when(pl.program_id(0) == 0)
    def _():
        cnt_ref[...] = jnp.zeros_like(cnt_ref)

    xn = _rms_rows(x_ref[...], nw_ref[...])
    w = w_ref[...]
    xh = xn.astype(BF16)
    xl = (xn - xh.astype(F32)).astype(BF16)
    wh = w.astype(BF16)
    wl = (w - wh.astype(F32)).astype(BF16)
    logits = (jnp.dot(xh, wh, preferred_element_type=F32) + jnp.dot(xl, wh, preferred_element_type=F32)
              + jnp.dot(xh, wl, preferred_element_type=F32)) + b_ref[...]
    lane = lax.broadcasted_iota(jnp.int32, logits.shape, 1).astype(F32)
    neg = -jnp.inf
    big = float(LANES)

    def first_argmax(vals):
        top = jnp.max(vals, axis=-1, keepdims=True)
        idx = jnp.min(jnp.where(vals == top, lane, big), axis=-1, keepdims=True)
        return top, idx

    gl = jnp.where(lane < N_GROUPS, logits, neg)
    gmax, gsel = first_argmax(gl)
    g_w = 1.0 / jnp.sum(jnp.exp(gl - gmax), axis=-1, keepdims=True)
    lo = N_GROUPS + EXPERTS_PER_GROUP * gsel
    el = jnp.where((lane >= lo) & (lane < lo + EXPERTS_PER_GROUP), logits, neg)
    t1, i1 = first_argmax(el)
    t2, i2 = first_argmax(jnp.where(lane == i1, neg, el))
    e = jnp.exp(t2 - t1)
    w1 = g_w / (1.0 + e)
    w2 = g_w * e / (1.0 + e)
    e1 = i1 - N_GROUPS
    e2 = i2 - N_GROUPS
    o_ref[...] = jnp.where(lane == 0, e1, jnp.where(lane == 1, e2, jnp.where(lane == 2, w1, jnp.where(lane == 3, w2, 0.0))))
    hits = jnp.where(lane == e1, 1.0, 0.0) + jnp.where(lane == e2, 1.0, 0.0)
    cnt_ref[...] += jnp.sum(hits, axis=0, keepdims=True)


def _router(h, ffn_norm_w, w_group, b_group, w_router, b_router, layer):
    t, d = h.shape
    tm = _row_tile(t, 256)
    pad = LANES - N_GROUPS - N_EXPERTS
    w = jnp.concatenate([w_group[layer], w_router[layer], jnp.zeros((d, pad), F32)], axis=1)
    b = jnp.concatenate([b_group[layer], b_router[layer], jnp.zeros((pad,), F32)])[None, :]
    return pl.pallas_call(
        _router_body,
        grid=(t // tm,),
        in_specs=[
            pl.BlockSpec((tm, d), lambda i: (i, 0)),
            pl.BlockSpec((None, 1, d), lambda i: (layer, 0, 0)),
            pl.BlockSpec((d, LANES), lambda i: (0, 0)),
            pl.BlockSpec((1, LANES), lambda i: (0, 0)),
        ],
        out_specs=[pl.BlockSpec((tm, LANES), lambda i: (i, 0)), pl.BlockSpec((1, LANES), lambda i: (0, 0))],
        out_shape=[jax.ShapeDtypeStruct((t, LANES), F32), jax.ShapeDtypeStruct((1, LANES), F32)],
        compiler_params=_cparams(("arbitrary",)),
        name="router",
    )(h, ffn_norm_w.reshape(-1, 1, d), w, b)


def _dispatch_tables(route, counts, t):
    tm = MOE_TILE
    a = t * TOP_K
    n_tiles = a // tm + N_EXPERTS
    eid = route[:, :TOP_K].astype(jnp.int32).reshape(a)
    wts = route[:, TOP_K:2 * TOP_K].reshape(a)
    order = jnp.argsort(eid, stable=True).astype(jnp.int32)
    counts = counts[0, :N_EXPERTS].astype(jnp.int32)
    seg_start = jnp.cumsum(counts) - counts
    padded = (counts + tm - 1) // tm * tm
    pad_end = jnp.cumsum(padded)
    pad_start = pad_end - padded
    n_valid = (pad_end[-1] // tm).astype(jnp.int32)
    tile = jnp.arange(n_tiles, dtype=jnp.int32)
    tile_expert = jnp.sum((pad_end[None, :] <= (tile * tm)[:, None]).astype(jnp.int32), axis=1)
    tile_expert = jnp.minimum(tile_expert, N_EXPERTS - 1)
    tile_expert = jnp.where(tile < n_valid, tile_expert, tile_expert[jnp.maximum(n_valid - 1, 0)])
    first_row = tile * tm - pad_start[tile_expert]
    n_rows = jnp.where(tile < n_valid, jnp.clip(counts[tile_expert] - first_row, 0, tm), 0).astype(jnp.int32)
    within = jnp.arange(tm, dtype=jnp.int32)[None, :]
    valid = within < n_rows[:, None]
    src = order[jnp.clip((seg_start[tile_expert] + first_row)[:, None] + within, 0, a - 1)]
    row_tok = jnp.where(valid, src // TOP_K, 0).reshape(n_tiles, 1, tm)
    row_w = jnp.where(valid, wts[src], 0.0).reshape(n_tiles * tm, 1)
    row_dst = jnp.where(valid, (src % TOP_K) * t + src // TOP_K, 0).reshape(n_tiles, 1, tm)
    return tile_expert, n_valid.reshape(1), n_rows, row_tok, row_w, row_dst


def _cast_rows(src_ref, dst_ref, rc):
    def one(r):
        dst_ref[r, :] = src_ref[r, :].astype(dst_ref.dtype)
    _for_row_chunks(src_ref.shape[0], rc, one)


def _expert_changed(te_ref, i):
    return (i == 0) | (te_ref[i] != te_ref[jnp.maximum(i - 1, 0)])


def _gateup_body(te_ref, nv_ref, rt_ref, rtn_ref, x_hbm, nw_ref, wg_ref, wu_ref, o_ref, xbuf, wgb, wub, sem, *, tm):
    i = pl.program_id(0)
    nv = nv_ref[0]
    slot = i % 2

    def issue(idx_ref, sl):
        def body(r, carry):
            pltpu.make_async_copy(x_hbm.at[pl.ds(idx_ref[0, r], 1), :], xbuf.at[sl, pl.ds(r, 1), :], sem.at[sl]).start()
            return carry
        lax.fori_loop(0, tm, body, 0, unroll=DMA_ISSUE_UNROLL)

    @pl.when((i == 0) & (nv > 0))
    def _():
        issue(rt_ref, 0)

    @pl.when(i + 1 < nv)
    def _():
        issue(rtn_ref, 1 - slot)

    @pl.when(i < nv)
    def _():
        pltpu.make_async_copy(x_hbm.at[pl.ds(0, tm), :], xbuf.at[slot], sem.at[slot]).wait()

        @pl.when(_expert_changed(te_ref, i))
        def _():
            _cast_rows(wg_ref, wgb, 256)
            _cast_rows(wu_ref, wub, 256)

        xn = _rms_rows(xbuf[slot], nw_ref[...]).astype(BF16)
        g = jnp.dot(xn, wgb[...], preferred_element_type=F32)
        u = jnp.dot(xn, wub[...], preferred_element_type=F32)
        o_ref[...] = (_silu(g) * u).astype(BF16)

    @pl.when(i >= nv)
    def _():
        o_ref[...] = jnp.zeros_like(o_ref)


def _expert_gateup(h, ffn_norm_w, w_gate, w_up, tile_expert, n_valid, row_tok, layer):
    t, d = h.shape
    tm = MOE_TILE
    n_tiles = row_tok.shape[0]
    de = w_gate.shape[-1]
    smem_rows = lambda f: pl.BlockSpec((None, 1, tm), f, memory_space=pltpu.SMEM)
    wspec = pl.BlockSpec((None, None, d, de), lambda i, te, nv: (layer, te[i], 0, 0))
    grid_spec = pltpu.PrefetchScalarGridSpec(
        num_scalar_prefetch=2,
        grid=(n_tiles,),
        in_specs=[
            smem_rows(lambda i, te, nv: (i, 0, 0)),
            smem_rows(lambda i, te, nv: (jnp.minimum(i + 1, n_tiles - 1), 0, 0)),
            pl.BlockSpec(memory_space=pl.ANY),
            pl.BlockSpec((None, 1, d), lambda i, te, nv: (layer, 0, 0)),
            wspec,
            wspec,
        ],
        out_specs=pl.BlockSpec((tm, de), lambda i, te, nv: (i, 0)),
        scratch_shapes=[pltpu.VMEM((2, tm, d), F32), pltpu.VMEM((d, de), BF16), pltpu.VMEM((d, de), BF16),
                        pltpu.SemaphoreType.DMA((2,))],
    )
    return pl.pallas_call(
        functools.partial(_gateup_body, tm=tm),
        grid_spec=grid_spec,
        out_shape=jax.ShapeDtypeStruct((n_tiles * tm, de), BF16),
        compiler_params=_cparams(("arbitrary",)),
        name="expert_gateup",
    )(tile_expert, n_valid, row_tok, row_tok, h, ffn_norm_w.reshape(-1, 1, d), w_gate, w_up)


def _down_body(te_ref, nv_ref, nr_ref, rd_ref, h_ref, rw_ref, wd_ref, y_hbm, ybuf, wdb, sem, *, tm):
    i = pl.program_id(0)
    nv = nv_ref[0]
    slot = i % 2

    def wait_rows(n, sl):
        n_bulk = pl.multiple_of((n // SUBLANES) * SUBLANES, SUBLANES)

        @pl.when(n_bulk > 0)
        def _():
            pltpu.make_async_copy(ybuf.at[sl, pl.ds(0, n_bulk), :], y_hbm.at[pl.ds(0, n_bulk), :], sem.at[sl]).wait()

        def one(r, carry):
            pltpu.make_async_copy(ybuf.at[sl, pl.ds(r, 1), :], y_hbm.at[pl.ds(r, 1), :], sem.at[sl]).wait()
            return carry
        lax.fori_loop(n_bulk, n, one, 0)

    @pl.when(i < nv)
    def _():
        @pl.when(_expert_changed(te_ref, i))
        def _():
            _cast_rows(wd_ref, wdb, 256)

        ybuf[slot] = jnp.dot(h_ref[...], wdb[...], preferred_element_type=F32) * rw_ref[...]

        @pl.when(i >= 1)
        def _():
            wait_rows(nr_ref[jnp.maximum(i - 1, 0)], 1 - slot)

        n = nr_ref[i]

        def body(r, carry):
            pltpu.make_async_copy(ybuf.at[slot, pl.ds(r, 1), :], y_hbm.at[pl.ds(rd_ref[0, r], 1), :], sem.at[slot]).start()
            return carry
        lax.fori_loop(0, n, body, 0)

        @pl.when(i == nv - 1)
        def _():
            wait_rows(n, slot)


def _expert_down(hid, w_down, tile_expert, n_valid, n_rows, row_dst, row_w, t, layer):
    tm = MOE_TILE
    n_tiles = row_dst.shape[0]
    de, d = w_down.shape[-2:]
    grid_spec = pltpu.PrefetchScalarGridSpec(
        num_scalar_prefetch=3,
        grid=(n_tiles,),
        in_specs=[
            pl.BlockSpec((None, 1, tm), lambda i, te, nv, nr: (i, 0, 0), memory_space=pltpu.SMEM),
            pl.BlockSpec((tm, de), lambda i, te, nv, nr: (i, 0)),
            pl.BlockSpec((tm, 1), lambda i, te, nv, nr: (i, 0)),
            pl.BlockSpec((None, None, de, d), lambda i, te, nv, nr: (layer, te[i], 0, 0)),
        ],
        out_specs=pl.BlockSpec(memory_space=pl.ANY),
        scratch_shapes=[pltpu.VMEM((2, tm, d), F32), pltpu.VMEM((de, d), BF16), pltpu.SemaphoreType.DMA((2,))],
    )
    return pl.pallas_call(
        functools.partial(_down_body, tm=tm),
        grid_spec=grid_spec,
        out_shape=jax.ShapeDtypeStruct((TOP_K * t, d), F32),
        compiler_params=_cparams(("arbitrary",)),
        name="expert_down",
    )(tile_expert, n_valid, n_rows, row_dst, hid, row_w, w_down)


def _combine_body(h_ref, y0_ref, y1_ref, nw_ref, o_ref, *, final):
    s = h_ref[...] + (y0_ref[...] + y1_ref[...])
    o_ref[...] = _rms_rows(s, nw_ref[...]) if final else s


def _combine(h, y, final_norm_w, final):
    t, d = h.shape
    tm = _row_tile(t, 256)
    nb = t // tm
    return pl.pallas_call(
        functools.partial(_combine_body, final=final),
        grid=(nb,),
        in_specs=[
            pl.BlockSpec((tm, d), lambda i: (i, 0)),
            pl.BlockSpec((tm, d), lambda i: (i, 0)),
            pl.BlockSpec((tm, d), lambda i: (i + nb, 0)),
            pl.BlockSpec((1, d), lambda i: (0, 0)),
        ],
        out_specs=pl.BlockSpec((tm, d), lambda i: (i, 0)),
        out_shape=jax.ShapeDtypeStruct((t, d), F32),
        compiler_params=_cparams(("parallel",)),
        name="combine_final" if final else "combine",
    )(h, y, y, final_norm_w.reshape(1, d))


def _hgrn_lower_bounds(lb_param):
    p = jax.nn.softmax(lb_param.astype(F32), axis=1)
    return jnp.cumsum(p, axis=1) - p[:, :1]


def _trunk(x, seq_lens, mix_norm_w, w_in, hgrn_lb, hgrn_out_norm_w, q_norm_w, k_norm_w, attn_out_norm_w, w_out,
           ffn_norm_w, w_group, b_group, w_router, b_router, w_gate_e, w_up_e, w_down_e, final_norm_w):
    t = x.shape[0]
    depth = w_in.shape[0]
    lb = _hgrn_lower_bounds(hgrn_lb).reshape(2, depth, N_HGRN_HEADS, 1, HEAD)
    cos, sin = _rope_tables(max(seq_lens))
    h = x
    for l in range(depth):
        proj = _norm_inproj(h, mix_norm_w, w_in, l)
        qh, kh, vh = _qkprep(proj, cos, sin, q_norm_w, k_norm_w, l, seq_lens)
        attn = _attention(qh, kh, vh, seq_lens)
        o_f = _hgrn(proj, lb[0, l], COL_FF, seq_lens, rev=False)
        o_b = _hgrn(proj, lb[1, l], COL_FB, seq_lens, rev=True)
        h = _merge_outproj(o_f, o_b, proj, attn, h, hgrn_out_norm_w, attn_out_norm_w, w_out, l)
        route, counts = _router(h, ffn_norm_w, w_group, b_group, w_router, b_router, l)
        tile_expert, n_valid, n_rows, row_tok, row_w, row_dst = _dispatch_tables(route, counts, t)
        hid = _expert_gateup(h, ffn_norm_w, w_gate_e, w_up_e, tile_expert, n_valid, row_tok, l)
        y = _expert_down(hid, w_down_e, tile_expert, n_valid, n_rows, row_dst, row_w, t, l)
        h = _combine(h, y, final_norm_w, final=(l == depth - 1))
    return h


def kernel(x_prompt, x_sample, mix_norm_w, w_in, hgrn_lb, hgrn_out_norm_w, q_norm_w, k_norm_w, attn_out_norm_w, w_out,
           ffn_norm_w, w_group, b_group, w_router, b_router, w_gate_e, w_up_e, w_down_e, final_norm_w):
    bp, lp, d = x_prompt.shape
    bs, ls, _ = x_sample.shape
    seq_lens = (lp,) * bp + (ls,) * bs
    x = jnp.concatenate([x_prompt.reshape(bp * lp, d), x_sample.reshape(bs * ls, d)], axis=0)
    y = _trunk(x, seq_lens, mix_norm_w, w_in, hgrn_lb, hgrn_out_norm_w, q_norm_w, k_norm_w, attn_out_norm_w, w_out,
               ffn_norm_w, w_group, b_group, w_router, b_router, w_gate_e, w_up_e, w_down_e, final_norm_w)
    return y[:bp * lp].reshape(bp, lp, d), y[bp * lp:].reshape(bs, ls, d)
```

```python
import functools

import numpy as np
import jax
import jax.numpy as jnp
from jax import lax
from jax.experimental import pallas as pl
from jax.experimental.pallas import tpu as pltpu

F32 = jnp.float32
BF16 = jnp.bfloat16

D_MODEL = 2048
DEPTH = 4
N_HGRN_HEADS = 8
HEAD = 128
HGRN_WIDTH = N_HGRN_HEADS * HEAD
N_Q_HEADS = 8
N_KV_HEADS = 2
KV_GROUP = N_Q_HEADS // N_KV_HEADS
ATT_WIDTH = N_Q_HEADS * HEAD
KV_WIDTH = N_KV_HEADS * HEAD
GRID_W = 64
ROPE_THETA = 10000.0
ROPE_AXIS_DIM = HEAD // 2
IN_COLS = 5 * HGRN_WIDTH + ATT_WIDTH + 2 * KV_WIDTH
COL_QH, COL_FF, COL_FB, COL_IH, COL_GH = (i * HGRN_WIDTH for i in range(5))
COL_QA = 5 * HGRN_WIDTH
COL_KA = COL_QA + ATT_WIDTH
COL_VA = COL_KA + KV_WIDTH
N_GROUPS = 4
EXPERTS_PER_GROUP = 8
N_EXPERTS = N_GROUPS * EXPERTS_PER_GROUP
TOP_K = 2
D_EXPERT = 1024
EPS = 1e-6

LANES = 128
SUBLANES = 8
CHUNK = 128
HGRN_UNROLL = 4
GROUP_LOG_BOUND = 10.0
VMEM_LIMIT = 56 * 1024 * 1024
MOE_TILE = 256
ATTN_Q_BLOCK = 1024
ATTN_CHAIN_ROWS = 512
MOE_COL_BLOCK = 256
DMA_ISSUE_UNROLL = 32

_NT = (((1,), (1,)), ((), ()))


def _cparams(sem):
    return pltpu.CompilerParams(dimension_semantics=sem, vmem_limit_bytes=VMEM_LIMIT)


def _rms_rows(x, w):
    ms = jnp.mean(x * x, axis=-1, keepdims=True)
    return x * lax.rsqrt(ms + EPS) * w


def _silu(x):
    return x * jax.nn.sigmoid(x)


def _row_tile(t, pref):
    while t % pref:
        pref //= 2
    return pref


def _for_row_chunks(n_rows, rc, fn):
    def body(c, carry):
        fn(pl.ds(pl.multiple_of(c * rc, rc), rc))
        return carry
    lax.fori_loop(0, n_rows // rc, body, 0)


def _norm_inproj_body(x_ref, nw_ref, w_ref, o_ref, xn_ref, *, tm, rc):
    @pl.when(pl.program_id(1) == 0)
    def _():
        def one(r):
            xn_ref[r, :] = _rms_rows(x_ref[r, :], nw_ref[...]).astype(BF16)
        _for_row_chunks(tm, rc, one)

    o_ref[...] = jnp.dot(xn_ref[...], w_ref[...].astype(BF16), preferred_element_type=F32)


def _norm_inproj(x, norm_w, w_in, layer):
    t, d = x.shape
    n = w_in.shape[-1]
    tm = _row_tile(t, 1024)
    tn = 512
    return pl.pallas_call(
        functools.partial(_norm_inproj_body, tm=tm, rc=min(256, tm)),
        grid=(t // tm, n // tn),
        in_specs=[
            pl.BlockSpec((tm, d), lambda i, j: (i, 0)),
            pl.BlockSpec((None, 1, d), lambda i, j: (layer, 0, 0)),
            pl.BlockSpec((None, d, tn), lambda i, j: (layer, 0, j)),
        ],
        out_specs=pl.BlockSpec((tm, tn), lambda i, j: (i, j)),
        out_shape=jax.ShapeDtypeStruct((t, n), F32),
        scratch_shapes=[pltpu.VMEM((tm, d), BF16)],
        compiler_params=_cparams(("parallel", "arbitrary")),
        name="norm_inproj",
    )(x, norm_w.reshape(-1, 1, d), w_in)


def _qkprep_body(pos_ref, q_ref, k_ref, v_ref, c_ref, s_ref, qw_ref, kw_ref, qo_ref, ko_ref, vo_ref):
    del pos_ref
    cos = c_ref[...]
    sin = s_ref[...]
    lane = lax.broadcasted_iota(jnp.int32, cos.shape, 1)
    first_half = (lane & (ROPE_AXIS_DIM // 2)) == 0
    scale = float(HEAD ** -0.5 * np.log2(np.e))

    def rope(x):
        partner = jnp.where(first_half, pltpu.roll(x, LANES - ROPE_AXIS_DIM // 2, 1),
                            pltpu.roll(x, ROPE_AXIS_DIM // 2, 1))
        return x * cos + partner * sin

    for h in range(N_Q_HEADS):
        sl = slice(h * HEAD, (h + 1) * HEAD)
        qo_ref[:, sl] = (rope(_rms_rows(q_ref[:, sl], qw_ref[...])) * scale).astype(BF16)
    for h in range(N_KV_HEADS):
        sl = slice(h * HEAD, (h + 1) * HEAD)
        ko_ref[:, sl] = rope(_rms_rows(k_ref[:, sl], kw_ref[...])).astype(BF16)
    vo_ref[...] = v_ref[...].astype(BF16)


def _rope_tables(max_len):
    pos = jnp.arange(max_len, dtype=jnp.int32)
    row = (pos // GRID_W).astype(F32)
    col = (pos % GRID_W).astype(F32)
    inv_freq = ROPE_THETA ** (-jnp.arange(0, ROPE_AXIS_DIM, 2, dtype=F32) / ROPE_AXIS_DIM)
    ang_r = row[:, None] * inv_freq
    ang_c = col[:, None] * inv_freq
    cos = jnp.concatenate([jnp.cos(ang_r), jnp.cos(ang_r), jnp.cos(ang_c), jnp.cos(ang_c)], axis=-1)
    sin = jnp.concatenate([-jnp.sin(ang_r), jnp.sin(ang_r), -jnp.sin(ang_c), jnp.sin(ang_c)], axis=-1)
    return cos, sin


def _qkprep(proj, cos, sin, q_norm_w, k_norm_w, layer, seq_lens):
    t = proj.shape[0]
    tp = min(256, min(seq_lens))
    pos_blk = np.concatenate([np.arange(l // tp) for l in seq_lens]).astype(np.int32)
    grid_spec = pltpu.PrefetchScalarGridSpec(
        num_scalar_prefetch=1,
        grid=(t // tp,),
        in_specs=[
            pl.BlockSpec((tp, ATT_WIDTH), lambda i, p: (i, COL_QA // ATT_WIDTH)),
            pl.BlockSpec((tp, KV_WIDTH), lambda i, p: (i, COL_KA // KV_WIDTH)),
            pl.BlockSpec((tp, KV_WIDTH), lambda i, p: (i, COL_VA // KV_WIDTH)),
            pl.BlockSpec((tp, HEAD), lambda i, p: (p[i], 0)),
            pl.BlockSpec((tp, HEAD), lambda i, p: (p[i], 0)),
            pl.BlockSpec((None, 1, HEAD), lambda i, p: (layer, 0, 0)),
            pl.BlockSpec((None, 1, HEAD), lambda i, p: (layer, 0, 0)),
        ],
        out_specs=[
            pl.BlockSpec((tp, ATT_WIDTH), lambda i, p: (i, 0)),
            pl.BlockSpec((tp, KV_WIDTH), lambda i, p: (i, 0)),
            pl.BlockSpec((tp, KV_WIDTH), lambda i, p: (i, 0)),
        ],
    )
    return pl.pallas_call(
        _qkprep_body,
        grid_spec=grid_spec,
        out_shape=[jax.ShapeDtypeStruct((t, ATT_WIDTH), BF16),
                   jax.ShapeDtypeStruct((t, KV_WIDTH), BF16),
                   jax.ShapeDtypeStruct((t, KV_WIDTH), BF16)],
        compiler_params=_cparams(("parallel",)),
        name="qkprep",
    )(jnp.asarray(pos_blk), proj, proj, proj, cos, sin,
      q_norm_w.reshape(-1, 1, HEAD), k_norm_w.reshape(-1, 1, HEAD))


def _attn_body(qt_ref, kt_ref, first_ref, last_ref, q_ref, k_ref, v_ref, o_ref, m_ref, acc_ref, *, tq):
    del qt_ref, kt_ref
    it = pl.program_id(1)

    @pl.when(first_ref[it] == 1)
    def _():
        m_ref[...] = jnp.full_like(m_ref, -jnp.inf)
        acc_ref[...] = jnp.zeros_like(acc_ref)

    k = k_ref[...]
    lane = lax.broadcasted_iota(jnp.int32, k.shape, 1)
    v1 = jnp.concatenate([v_ref[...], jnp.where(lane == 0, 1.0, 0.0).astype(BF16)], axis=1)
    rc = min(tq, ATTN_CHAIN_ROWS)
    for h, rb in [(h, rb) for h in range(KV_GROUP) for rb in range(tq // rc)]:
        rows = slice(h * tq + rb * rc, h * tq + (rb + 1) * rc)
        q = q_ref[rb * rc:(rb + 1) * rc, h * HEAD:(h + 1) * HEAD]
        s = lax.dot_general(q, k, _NT, preferred_element_type=F32)
        m_prev = m_ref[rows, :]
        m_new = jnp.maximum(m_prev, jnp.max(s, axis=-1, keepdims=True))
        alpha = jnp.exp2(m_prev - m_new)
        p = jnp.exp2(s - m_new[:, 0:1]).astype(BF16)
        pv = jnp.dot(p, v1, preferred_element_type=F32)
        acc = acc_ref[rows, :]
        acc_ref[rows, :] = jnp.concatenate([alpha * acc[:, :HEAD], alpha * acc[:, HEAD:]], axis=1) + pv
        m_ref[rows, :] = m_new

    @pl.when(last_ref[it] == 1)
    def _():
        for h in range(KV_GROUP):
            acc = acc_ref[h * tq:(h + 1) * tq, :]
            o_ref[:, h * HEAD:(h + 1) * HEAD] = acc[:, :HEAD] / acc[:, HEAD:HEAD + 1]


def _attention(qh, kh, vh, seq_lens):
    t = qh.shape[0]
    tq = min(ATTN_Q_BLOCK, min(seq_lens))
    tk = min(1024, min(seq_lens))
    qt, kt, first, last = [], [], [], []
    start = 0
    for l in seq_lens:
        nk = l // tk
        for qb in range(l // tq):
            for kb in range(nk):
                qt.append(start // tq + qb)
                kt.append(start // tk + kb)
                first.append(int(kb == 0))
                last.append(int(kb == nk - 1))
        start += l
    tabs = [jnp.asarray(np.asarray(a, np.int32)) for a in (qt, kt, first, last)]
    qw = KV_GROUP * HEAD
    grid_spec = pltpu.PrefetchScalarGridSpec(
        num_scalar_prefetch=4,
        grid=(N_KV_HEADS, len(qt)),
        in_specs=[
            pl.BlockSpec((tq, qw), lambda g, i, qt_, kt_, f_, l_: (qt_[i], g)),
            pl.BlockSpec((tk, HEAD), lambda g, i, qt_, kt_, f_, l_: (kt_[i], g)),
            pl.BlockSpec((tk, HEAD), lambda g, i, qt_, kt_, f_, l_: (kt_[i], g)),
        ],
        out_specs=pl.BlockSpec((tq, qw), lambda g, i, qt_, kt_, f_, l_: (qt_[i], g)),
        scratch_shapes=[pltpu.VMEM((KV_GROUP * tq, LANES), F32), pltpu.VMEM((KV_GROUP * tq, 2 * HEAD), F32)],
    )
    return pl.pallas_call(
        functools.partial(_attn_body, tq=tq),
        grid_spec=grid_spec,
        out_shape=jax.ShapeDtypeStruct((t, ATT_WIDTH), F32),
        compiler_params=_cparams(("parallel", "arbitrary")),
        name="attention",
    )(*tabs, qh, kh, vh)


_LEVEL_HALVES = (8, 16, 32, 64)


def _level_ref_groups(rev, hs):
    g = np.arange(CHUNK // SUBLANES)
    per = 2 * hs // SUBLANES
    return [int(x) for x in (g // per) * per + (hs // SUBLANES if rev else hs // SUBLANES - 1)]


def _hgrn_constants(rev):
    i = np.arange(CHUNK)
    cum = (i[None, :] >= i[:, None]) if rev else (i[None, :] <= i[:, None])
    wsel = (np.arange(SUBLANES * CHUNK)[:, None] // CHUNK) == (np.arange(CHUNK)[None, :] % SUBLANES)
    ii, jj = i[:, None], i[None, :]
    lvl = np.full((CHUNK, CHUNK), 5, np.int32)
    for n, hs in reversed(list(enumerate(_LEVEL_HALVES, start=1))):
        lvl[(ii // (2 * hs)) == (jj // (2 * hs))] = n
    lvl[(ii // SUBLANES) == (jj // SUBLANES)] = 0
    lvl[(jj < ii) if rev else (jj > ii)] = 5
    return (jnp.asarray(cum.astype(np.float32), BF16), jnp.asarray(wsel.astype(np.float32), BF16),
            jnp.asarray(lvl))


def _hgrn_chunk(rev, mild, q_raw, f_raw, v, lbv, cum_ref, wsel_ref, lvl, st_ref):
    groups = CHUNK // SUBLANES
    edge = 0 if rev else CHUNK - 1
    n_lvl = len(_LEVEL_HALVES)
    q = _silu(q_raw)
    f = lbv + (1.0 - lbv) * jax.nn.sigmoid(f_raw)
    k = 1.0 - f
    lf = jnp.log(f)
    hi = lf.astype(BF16)
    r1 = lf - hi.astype(F32)
    mid = r1.astype(BF16)
    lo = (r1 - mid.astype(F32)).astype(BF16)
    cs = jnp.dot(cum_ref[...], jnp.concatenate([hi, mid, lo], axis=1), preferred_element_type=F32)
    b = cs[:, :HEAD] + cs[:, HEAD:2 * HEAD] + cs[:, 2 * HEAD:]
    b3 = b.reshape(groups, SUBLANES, HEAD)
    g_first = jnp.broadcast_to(b3[:, 0:1, :], b3.shape)
    g_last = jnp.broadcast_to(b3[:, SUBLANES - 1:SUBLANES, :], b3.shape)
    near, far = (g_last, g_first) if rev else (g_first, g_last)

    level_scores = []
    for hs in _LEVEL_HALVES:
        b_ref = jnp.concatenate([far[g:g + 1] for g in _level_ref_groups(rev, hs)], axis=0)
        a = jnp.exp(-jnp.abs(b - b_ref.reshape(CHUNK, HEAD)))
        level_scores.append(lax.dot_general((q * a).astype(BF16), (k * a).astype(BF16), _NT,
                                            preferred_element_type=F32))

    def group_scores_factored():
        d = b - near.reshape(CHUNK, HEAD)
        return lax.dot_general((q * jnp.exp(d)).astype(BF16), (k * jnp.exp(-d)).astype(BF16), _NT,
                               preferred_element_type=F32)

    def group_scores_pairwise():
        sub = lax.broadcasted_iota(jnp.int32, (groups, SUBLANES, HEAD), 1)
        q3 = q.reshape(groups, SUBLANES, HEAD)
        k3 = k.reshape(groups, SUBLANES, HEAD)
        slabs = []
        for j in range(SUBLANES):
            bj = jnp.broadcast_to(b3[:, j:j + 1, :], b3.shape)
            kj = jnp.broadcast_to(k3[:, j:j + 1, :], k3.shape)
            ok = (sub <= j) if rev else (sub >= j)
            e = jnp.exp(jnp.where(ok, b3 - bj, -jnp.inf))
            slabs.append((q3 * kj * e).reshape(CHUNK, HEAD).astype(BF16))
        return jnp.dot(jnp.concatenate(slabs, axis=1), wsel_ref[...], preferred_element_type=F32)

    scores = group_scores_factored() if mild else group_scores_pairwise()
    scores = jnp.where(lvl == 0, scores, 0.0)
    for n, s in enumerate(level_scores, start=1):
        scores = jnp.where(lvl == n, s, scores)
    o_intra = jnp.dot(scores.astype(BF16), v.astype(BF16), preferred_element_type=F32)

    st = st_ref[...]
    b_edge = b[edge:edge + 1, :]
    o_inter = lax.dot_general((q * jnp.exp(b)).astype(BF16), st.astype(BF16), _NT, preferred_element_type=F32)
    k_end = (k * jnp.exp(b_edge - b)).astype(BF16)
    st_ref[...] = st * jnp.exp(b_edge) + jnp.dot(v.T.astype(BF16), k_end, preferred_element_type=F32)
    return o_intra + o_inter


def _hgrn_body(rsf_ref, rsb_ref, qf_ref, ff_ref, vf_ref, qb_ref, fb_ref, vb_ref, lbf_ref, lbb_ref,
               cumf_ref, cumb_ref, wsel_ref, lvlf_ref, lvlb_ref, of_ref, ob_ref, stf_ref, stb_ref, *, nchunk, nblk):
    it = pl.program_id(1)

    @pl.when(rsf_ref[it] == 1)
    def _():
        stf_ref[...] = jnp.zeros_like(stf_ref)

    @pl.when(rsb_ref[nblk - 1 - it] == 1)
    def _():
        stb_ref[...] = jnp.zeros_like(stb_ref)

    lbf = lbf_ref[...]
    lbb = lbb_ref[...]
    lvlf = lvlf_ref[...]
    lvlb = lvlb_ref[...]

    def run(mild):
        def chunk(ci, carry):
            rf = pl.ds(pl.multiple_of(ci * CHUNK, CHUNK), CHUNK)
            rb = pl.ds(pl.multiple_of((nchunk - 1 - ci) * CHUNK, CHUNK), CHUNK)
            of_ref[rf, :] = _hgrn_chunk(False, mild, qf_ref[rf, :], ff_ref[rf, :], vf_ref[rf, :], lbf, cumf_ref,
                                        wsel_ref, lvlf, stf_ref)
            ob_ref[rb, :] = _hgrn_chunk(True, mild, qb_ref[rb, :], fb_ref[rb, :], vb_ref[rb, :], lbb, cumb_ref,
                                        wsel_ref, lvlb, stb_ref)
            return carry
        lax.fori_loop(0, nchunk, chunk, 0, unroll=HGRN_UNROLL if mild else 1)

    mild = jnp.minimum(jnp.min(ff_ref[...]), jnp.min(fb_ref[...])) >= -(GROUP_LOG_BOUND - 1.0)

    @pl.when(mild)
    def _():
        run(True)

    @pl.when(jnp.logical_not(mild))
    def _():
        run(False)


def _hgrn(proj, lb_f, lb_b, seq_lens):
    t = proj.shape[0]
    lblk = min(512, min(seq_lens))
    nblk = t // lblk
    starts = np.cumsum([0] + list(seq_lens))
    reset_f = np.zeros((nblk,), np.int32)
    reset_b = np.zeros((nblk,), np.int32)
    for s, l in zip(starts[:-1], seq_lens):
        reset_f[s // lblk] = 1
        reset_b[(s + l) // lblk - 1] = 1
    cum_f, wsel, lvl_f = _hgrn_constants(False)
    cum_b, _, lvl_b = _hgrn_constants(True)

    def col(off, rev):
        return lambda h, i, rf, rb: (nblk - 1 - i if rev else i, off // HEAD + h)

    const = lambda h, i, rf, rb: (0, 0)
    head = lambda h, i, rf, rb: (h, 0, 0)
    grid_spec = pltpu.PrefetchScalarGridSpec(
        num_scalar_prefetch=2,
        grid=(N_HGRN_HEADS, nblk),
        in_specs=[
            pl.BlockSpec((lblk, HEAD), col(COL_QH, False)),
            pl.BlockSpec((lblk, HEAD), col(COL_FF, False)),
            pl.BlockSpec((lblk, HEAD), col(COL_IH, False)),
            pl.BlockSpec((lblk, HEAD), col(COL_QH, True)),
            pl.BlockSpec((lblk, HEAD), col(COL_FB, True)),
            pl.BlockSpec((lblk, HEAD), col(COL_IH, True)),
            pl.BlockSpec((None, 1, HEAD), head),
            pl.BlockSpec((None, 1, HEAD), head),
            pl.BlockSpec(cum_f.shape, const),
            pl.BlockSpec(cum_b.shape, const),
            pl.BlockSpec(wsel.shape, const),
            pl.BlockSpec(lvl_f.shape, const),
            pl.BlockSpec(lvl_b.shape, const),
        ],
        out_specs=[pl.BlockSpec((lblk, HEAD), lambda h, i, rf, rb: (i, h)),
                   pl.BlockSpec((lblk, HEAD), lambda h, i, rf, rb: (nblk - 1 - i, h))],
        scratch_shapes=[pltpu.VMEM((HEAD, HEAD), F32), pltpu.VMEM((HEAD, HEAD), F32)],
    )
    return pl.pallas_call(
        functools.partial(_hgrn_body, nchunk=lblk // CHUNK, nblk=nblk),
        grid_spec=grid_spec,
        out_shape=[jax.ShapeDtypeStruct((t, HGRN_WIDTH), F32), jax.ShapeDtypeStruct((t, HGRN_WIDTH), F32)],
        compiler_params=_cparams(("parallel", "arbitrary")),
        name="hgrn",
    )(jnp.asarray(reset_f), jnp.asarray(reset_b), proj, proj, proj, proj, proj, proj, lb_f, lb_b,
      cum_f, cum_b, wsel, lvl_f, lvl_b)


def _merge_outproj_body(of_ref, ob_ref, g_ref, a_ref, x_ref, hw_ref, aw_ref, w_ref, o_ref, mg_ref, *, tm, rc):
    @pl.when(pl.program_id(1) == 0)
    def _():
        def one(r):
            for h in range(N_HGRN_HEADS):
                sl = slice(h * HEAD, (h + 1) * HEAD)
                o = of_ref[r, sl] + ob_ref[r, sl]
                mg_ref[r, sl] = (_rms_rows(o, hw_ref[...]) * _silu(g_ref[r, sl])).astype(BF16)
            mg_ref[r, HGRN_WIDTH:] = _rms_rows(a_ref[r, :], aw_ref[...]).astype(BF16)
        _for_row_chunks(tm, rc, one)

    o_ref[...] = x_ref[...] + jnp.dot(mg_ref[...], w_ref[...].astype(BF16), preferred_element_type=F32)


def _merge_outproj(o_f, o_b, proj, attn, x, hgrn_norm_w, attn_norm_w, w_out, layer):
    t, d = x.shape
    tm = _row_tile(t, 1024)
    tn = 512
    return pl.pallas_call(
        functools.partial(_merge_outproj_body, tm=tm, rc=min(256, tm)),
        grid=(t // tm, d // tn),
        in_specs=[
            pl.BlockSpec((tm, HGRN_WIDTH), lambda i, j: (i, 0)),
            pl.BlockSpec((tm, HGRN_WIDTH), lambda i, j: (i, 0)),
            pl.BlockSpec((tm, HGRN_WIDTH), lambda i, j: (i, COL_GH // HGRN_WIDTH)),
            pl.BlockSpec((tm, ATT_WIDTH), lambda i, j: (i, 0)),
            pl.BlockSpec((tm, tn), lambda i, j: (i, j)),
            pl.BlockSpec((None, 1, HEAD), lambda i, j: (layer, 0, 0)),
            pl.BlockSpec((None, 1, ATT_WIDTH), lambda i, j: (layer, 0, 0)),
            pl.BlockSpec((None, d, tn), lambda i, j: (layer, 0, j)),
        ],
        out_specs=pl.BlockSpec((tm, tn), lambda i, j: (i, j)),
        out_shape=jax.ShapeDtypeStruct((t, d), F32),
        scratch_shapes=[pltpu.VMEM((tm, d), BF16)],
        compiler_params=_cparams(("parallel", "arbitrary")),
        name="merge_outproj",
    )(o_f, o_b, proj, attn, x, hgrn_norm_w.reshape(-1, 1, HEAD), attn_norm_w.reshape(-1, 1, ATT_WIDTH), w_out)


def _router_body(x_ref, nw_ref, w_ref, b_ref, o_ref, cnt_ref):
    @pl.when(pl.program_id(0) == 0)
    def _():
        cnt_ref[...] = jnp.zeros_like(cnt_ref)

    xn = _rms_rows(x_ref[...], nw_ref[...])
    w = w_ref[...]
    xh = xn.astype(BF16)
    xl = (xn - xh.astype(F32)).astype(BF16)
    wh = w.astype(BF16)
    wl = (w - wh.astype(F32)).astype(BF16)
    logits = (jnp.dot(xh, wh, preferred_element_type=F32) + jnp.dot(xl, wh, preferred_element_type=F32)
              + jnp.dot(xh, wl, preferred_element_type=F32)) + b_ref[...]
    lane = lax.broadcasted_iota(jnp.int32, logits.shape, 1).astype(F32)
    neg = -jnp.inf
    big = float(LANES)

    def first_argmax(vals):
        top = jnp.max(vals, axis=-1, keepdims=True)
        idx = jnp.min(jnp.where(vals == top, lane, big), axis=-1, keepdims=True)
        return top, idx

    gl = jnp.where(lane < N_GROUPS, logits, neg)
    gmax, gsel = first_argmax(gl)
    g_w = 1.0 / jnp.sum(jnp.exp(gl - gmax), axis=-1, keepdims=True)
    lo = N_GROUPS + EXPERTS_PER_GROUP * gsel
    el = jnp.where((lane >= lo) & (lane < lo + EXPERTS_PER_GROUP), logits, neg)
    t1, i1 = first_argmax(el)
    t2, i2 = first_argmax(jnp.where(lane == i1, neg, el))
    e = jnp.exp(t2 - t1)
    w1 = g_w / (1.0 + e)
    w2 = g_w * e / (1.0 + e)
    e1 = i1 - N_GROUPS
    e2 = i2 - N_GROUPS
    o_ref[...] = jnp.where(lane == 0, e1, jnp.where(lane == 1, e2, jnp.where(lane == 2, w1, jnp.where(lane == 3, w2, 0.0))))
    hits = jnp.where(lane == e1, 1.0, 0.0) + jnp.where(lane == e2, 1.0, 0.0)
    cnt_ref[...] += jnp.sum(hits, axis=0, keepdims=True)


def _router(h, ffn_norm_w, w_group, b_group, w_router, b_router, layer):
    t, d = h.shape
    tm = _row_tile(t, 256)
    pad = LANES - N_GROUPS - N_EXPERTS
    w = jnp.concatenate([w_group[layer], w_router[layer], jnp.zeros((d, pad), F32)], axis=1)
    b = jnp.concatenate([b_group[layer], b_router[layer], jnp.zeros((pad,), F32)])[None, :]
    return pl.pallas_call(
        _router_body,
        grid=(t // tm,),
        in_specs=[
            pl.BlockSpec((tm, d), lambda i: (i, 0)),
            pl.BlockSpec((None, 1, d), lambda i: (layer, 0, 0)),
            pl.BlockSpec((d, LANES), lambda i: (0, 0)),
            pl.BlockSpec((1, LANES), lambda i: (0, 0)),
        ],
        out_specs=[pl.BlockSpec((tm, LANES), lambda i: (i, 0)), pl.BlockSpec((1, LANES), lambda i: (0, 0))],
        out_shape=[jax.ShapeDtypeStruct((t, LANES), F32), jax.ShapeDtypeStruct((1, LANES), F32)],
        compiler_params=_cparams(("arbitrary",)),
        name="router",
    )(h, ffn_norm_w.reshape(-1, 1, d), w, b)


def _dispatch_tables(route, counts, t):
    tm = MOE_TILE
    a = t * TOP_K
    n_tiles = a // tm + N_EXPERTS
    eid = route[:, :TOP_K].astype(jnp.int32).reshape(a)
    wts = route[:, TOP_K:2 * TOP_K].reshape(a)
    order = jnp.argsort(eid, stable=True).astype(jnp.int32)
    counts = counts[0, :N_EXPERTS].astype(jnp.int32)
    seg_start = jnp.cumsum(counts) - counts
    padded = (counts + tm - 1) // tm * tm
    pad_end = jnp.cumsum(padded)
    pad_start = pad_end - padded
    n_valid = (pad_end[-1] // tm).astype(jnp.int32)
    tile = jnp.arange(n_tiles, dtype=jnp.int32)
    tile_expert = jnp.sum((pad_end[None, :] <= (tile * tm)[:, None]).astype(jnp.int32), axis=1)
    tile_expert = jnp.minimum(tile_expert, N_EXPERTS - 1)
    tile_expert = jnp.where(tile < n_valid, tile_expert, tile_expert[jnp.maximum(n_valid - 1, 0)])
    first_row = tile * tm - pad_start[tile_expert]
    n_rows = jnp.where(tile < n_valid, jnp.clip(counts[tile_expert] - first_row, 0, tm), 0).astype(jnp.int32)
    within = jnp.arange(tm, dtype=jnp.int32)[None, :]
    valid = within < n_rows[:, None]
    src = order[jnp.clip((seg_start[tile_expert] + first_row)[:, None] + within, 0, a - 1)]
    row_tok = jnp.where(valid, src // TOP_K, 0).reshape(n_tiles, 1, tm)
    row_w = jnp.where(valid, wts[src], 0.0).reshape(n_tiles * tm, 1)
    row_dst = jnp.where(valid, (src % TOP_K) * t + src // TOP_K, TOP_K * t + within).reshape(n_tiles, 1, tm)
    return tile_expert, n_valid.reshape(1), row_tok, row_w, row_dst


def _cast_rows(src_ref, dst_ref, rc):
    def one(r):
        dst_ref[r, :] = src_ref[r, :].astype(dst_ref.dtype)
    _for_row_chunks(src_ref.shape[0], rc, one)


def _expert_changed(te_ref, i):
    return (i == 0) | (te_ref[i] != te_ref[jnp.maximum(i - 1, 0)])


def _gateup_body(te_ref, nv_ref, rt_ref, rtn_ref, x_hbm, nw_ref, wg_ref, wu_ref, o_ref, xbuf, xn_ref, wgb, wub, sem,
                 *, tm):
    i = pl.program_id(0)
    nv = nv_ref[0]
    slot = i % 2

    def row_copy(idx_ref, sl, r):
        return pltpu.make_async_copy(x_hbm.at[pl.ds(idx_ref[0, r], 1), :], xbuf.at[sl, pl.ds(r, 1), :], sem.at[sl])

    def wait_slot(sl):
        pltpu.make_async_copy(x_hbm.at[pl.ds(0, tm), :], xbuf.at[sl], sem.at[sl]).wait()

    @pl.when((i == 0) & (nv > 0))
    def _():
        def body(r, carry):
            row_copy(rt_ref, 0, r).start()
            return carry
        lax.fori_loop(0, tm, body, 0, unroll=DMA_ISSUE_UNROLL)

    @pl.when(i < nv)
    def _():
        wait_slot(slot)

        @pl.when(_expert_changed(te_ref, i))
        def _():
            _cast_rows(wg_ref, wgb, 256)
            _cast_rows(wu_ref, wub, 256)

        xn_ref[...] = _rms_rows(xbuf[slot], nw_ref[...]).astype(BF16)
        de = o_ref.shape[1]
        nblk = de // MOE_COL_BLOCK
        per = tm // nblk
        for j in range(nblk):
            for r in range(j * per, (j + 1) * per):
                row_copy(rtn_ref, 1 - slot, r).start()
            cols = slice(j * MOE_COL_BLOCK, (j + 1) * MOE_COL_BLOCK)
            g = jnp.dot(xn_ref[...], wgb[:, cols], preferred_element_type=F32)
            u = jnp.dot(xn_ref[...], wub[:, cols], preferred_element_type=F32)
            o_ref[:, cols] = (_silu(g) * u).astype(BF16)

        @pl.when(i == nv - 1)
        def _():
            wait_slot(1 - slot)

    @pl.when(i >= nv)
    def _():
        o_ref[...] = jnp.zeros_like(o_ref)


def _expert_gateup(h, ffn_norm_w, w_gate, w_up, tile_expert, n_valid, row_tok, layer):
    t, d = h.shape
    tm = MOE_TILE
    n_tiles = row_tok.shape[0]
    de = w_gate.shape[-1]
    smem_rows = lambda f: pl.BlockSpec((None, 1, tm), f, memory_space=pltpu.SMEM)
    wspec = pl.BlockSpec((None, None, d, de), lambda i, te, nv: (layer, te[i], 0, 0))
    grid_spec = pltpu.PrefetchScalarGridSpec(
        num_scalar_prefetch=2,
        grid=(n_tiles,),
        in_specs=[
            smem_rows(lambda i, te, nv: (i, 0, 0)),
            smem_rows(lambda i, te, nv: (jnp.minimum(i + 1, n_tiles - 1), 0, 0)),
            pl.BlockSpec(memory_space=pl.ANY),
            pl.BlockSpec((None, 1, d), lambda i, te, nv: (layer, 0, 0)),
            wspec,
            wspec,
        ],
        out_specs=pl.BlockSpec((tm, de), lambda i, te, nv: (i, 0)),
        scratch_shapes=[pltpu.VMEM((2, tm, d), F32), pltpu.VMEM((tm, d), BF16), pltpu.VMEM((d, de), BF16),
                        pltpu.VMEM((d, de), BF16), pltpu.SemaphoreType.DMA((2,))],
    )
    return pl.pallas_call(
        functools.partial(_gateup_body, tm=tm),
        grid_spec=grid_spec,
        out_shape=jax.ShapeDtypeStruct((n_tiles * tm, de), BF16),
        compiler_params=_cparams(("arbitrary",)),
        name="expert_gateup",
    )(tile_expert, n_valid, row_tok, row_tok, h, ffn_norm_w.reshape(-1, 1, d), w_gate, w_up)


def _down_body(te_ref, nv_ref, rd_ref, rdp_ref, h_ref, rw_ref, wd_ref, y_hbm, ybuf, wdb, sem, *, tm, spare_row0):
    i = pl.program_id(0)
    nv = nv_ref[0]
    slot = i % 2

    def row_copy(idx_ref, sl, r):
        return pltpu.make_async_copy(ybuf.at[sl, pl.ds(r, 1), :], y_hbm.at[pl.ds(idx_ref[0, r], 1), :], sem.at[sl])

    def wait_slot(sl):
        pltpu.make_async_copy(ybuf.at[sl], y_hbm.at[pl.ds(0, tm), :], sem.at[sl]).wait()

    @pl.when(i == 0)
    def _():
        ybuf[...] = jnp.zeros_like(ybuf)
        init = pltpu.make_async_copy(ybuf.at[0], y_hbm.at[pl.ds(spare_row0, tm), :], sem.at[0])
        init.start()
        init.wait()

    @pl.when(i < nv)
    def _():
        @pl.when(i >= 1)
        def _():
            wait_slot(slot)

        @pl.when(_expert_changed(te_ref, i))
        def _():
            _cast_rows(wd_ref, wdb, 256)

        d = ybuf.shape[2]
        nblk = d // MOE_COL_BLOCK
        per = tm // nblk
        for j in range(nblk):
            for r in range(j * per, (j + 1) * per):
                row_copy(rdp_ref, 1 - slot, r).start()
            cols = slice(j * MOE_COL_BLOCK, (j + 1) * MOE_COL_BLOCK)
            ybuf[slot, :, cols] = jnp.dot(h_ref[...], wdb[:, cols], preferred_element_type=F32) * rw_ref[...]

        @pl.when(i == nv - 1)
        def _():
            wait_slot(1 - slot)

            def body(r, carry):
                row_copy(rd_ref, slot, r).start()
                return carry
            lax.fori_loop(0, tm, body, 0, unroll=DMA_ISSUE_UNROLL)
            wait_slot(slot)


def _expert_down(hid, w_down, tile_expert, n_valid, row_dst, row_w, t, layer):
    tm = MOE_TILE
    n_tiles = row_dst.shape[0]
    de, d = w_down.shape[-2:]
    spare = TOP_K * t + jnp.arange(tm, dtype=jnp.int32).reshape(1, 1, tm)
    row_dst_prev = jnp.concatenate([spare, row_dst[:-1]], axis=0)
    smem_rows = pl.BlockSpec((None, 1, tm), lambda i, te, nv: (i, 0, 0), memory_space=pltpu.SMEM)
    grid_spec = pltpu.PrefetchScalarGridSpec(
        num_scalar_prefetch=2,
        grid=(n_tiles,),
        in_specs=[
            smem_rows,
            smem_rows,
            pl.BlockSpec((tm, de), lambda i, te, nv: (i, 0)),
            pl.BlockSpec((tm, 1), lambda i, te, nv: (i, 0)),
            pl.BlockSpec((None, None, de, d), lambda i, te, nv: (layer, te[i], 0, 0)),
        ],
        out_specs=pl.BlockSpec(memory_space=pl.ANY),
        scratch_shapes=[pltpu.VMEM((2, tm, d), F32), pltpu.VMEM((de, d), BF16), pltpu.SemaphoreType.DMA((2,))],
    )
    return pl.pallas_call(
        functools.partial(_down_body, tm=tm, spare_row0=TOP_K * t),
        grid_spec=grid_spec,
        out_shape=jax.ShapeDtypeStruct((TOP_K * t + tm, d), F32),
        compiler_params=_cparams(("arbitrary",)),
        name="expert_down",
    )(tile_expert, n_valid, row_dst, row_dst_prev, hid, row_w, w_down)


def _combine_body(h_ref, y0_ref, y1_ref, nw_ref, o_ref, *, final):
    s = h_ref[...] + (y0_ref[...] + y1_ref[...])
    o_ref[...] = _rms_rows(s, nw_ref[...]) if final else s


def _combine(h, y, final_norm_w, final):
    t, d = h.shape
    tm = _row_tile(t, 256)
    nb = t // tm
    return pl.pallas_call(
        functools.partial(_combine_body, final=final),
        grid=(nb,),
        in_specs=[
            pl.BlockSpec((tm, d), lambda i: (i, 0)),
            pl.BlockSpec((tm, d), lambda i: (i, 0)),
            pl.BlockSpec((tm, d), lambda i: (i + nb, 0)),
            pl.BlockSpec((1, d), lambda i: (0, 0)),
        ],
        out_specs=pl.BlockSpec((tm, d), lambda i: (i, 0)),
        out_shape=jax.ShapeDtypeStruct((t, d), F32),
        compiler_params=_cparams(("parallel",)),
        name="combine_final" if final else "combine",
    )(h, y, y, final_norm_w.reshape(1, d))


def _hgrn_lower_bounds(lb_param):
    p = jax.nn.softmax(lb_param.astype(F32), axis=1)
    return jnp.cumsum(p, axis=1) - p[:, :1]


def _trunk(x, seq_lens, mix_norm_w, w_in, hgrn_lb, hgrn_out_norm_w, q_norm_w, k_norm_w, attn_out_norm_w, w_out,
           ffn_norm_w, w_group, b_group, w_router, b_router, w_gate_e, w_up_e, w_down_e, final_norm_w):
    t = x.shape[0]
    depth = w_in.shape[0]
    lb = _hgrn_lower_bounds(hgrn_lb).reshape(2, depth, N_HGRN_HEADS, 1, HEAD)
    cos, sin = _rope_tables(max(seq_lens))
    h = x
    for l in range(depth):
        proj = _norm_inproj(h, mix_norm_w, w_in, l)
        qh, kh, vh = _qkprep(proj, cos, sin, q_norm_w, k_norm_w, l, seq_lens)
        attn = _attention(qh, kh, vh, seq_lens)
        o_f, o_b = _hgrn(proj, lb[0, l], lb[1, l], seq_lens)
        h = _merge_outproj(o_f, o_b, proj, attn, h, hgrn_out_norm_w, attn_out_norm_w, w_out, l)
        route, counts = _router(h, ffn_norm_w, w_group, b_group, w_router, b_router, l)
        tile_expert, n_valid, row_tok, row_w, row_dst = _dispatch_tables(route, counts, t)
        hid = _expert_gateup(h, ffn_norm_w, w_gate_e, w_up_e, tile_expert, n_valid, row_tok, l)
        y = _expert_down(hid, w_down_e, tile_expert, n_valid, row_dst, row_w, t, l)
        h = _combine(h, y, final_norm_w, final=(l == depth - 1))
    return h


def kernel(x_prompt, x_sample, mix_norm_w, w_in, hgrn_lb, hgrn_out_norm_w, q_norm_w, k_norm_w, attn_out_norm_w, w_out,
           ffn_norm_w, w_group, b_group, w_router, b_router, w_gate_e, w_up_e, w_down_e, final_norm_w):
    bp, lp, d = x_prompt.shape
    bs, ls, _ = x_sample.shape
    seq_lens = (lp,) * bp + (ls,) * bs
    x = jnp.concatenate([x_prompt.reshape(bp * lp, d), x_sample.reshape(bs * ls, d)], axis=0)
    y = _trunk(x, seq_lens, mix_norm_w, w_in, hgrn_lb, hgrn_out_norm_w, q_norm_w, k_norm_w, attn_out_norm_w, w_out,
               ffn_norm_w, w_group, b_group, w_router, b_router, w_gate_e, w_up_e, w_down_e, final_norm_w)
    return y[:bp * lp].reshape(bp, lp, d), y[bp * lp:].reshape(bs, ls, d)
```

```python
import functools

import numpy as np
import jax
import jax.numpy as jnp
from jax import lax
from jax.experimental import pallas as pl
from jax.experimental.pallas import tpu as pltpu

F32 = jnp.float32
BF16 = jnp.bfloat16

D_MODEL = 2048
DEPTH = 4
N_HGRN_HEADS = 8
HEAD = 128
HGRN_WIDTH = N_HGRN_HEADS * HEAD
N_Q_HEADS = 8
N_KV_HEADS = 2
KV_GROUP = N_Q_HEADS // N_KV_HEADS
ATT_WIDTH = N_Q_HEADS * HEAD
KV_WIDTH = N_KV_HEADS * HEAD
GRID_W = 64
ROPE_THETA = 10000.0
ROPE_AXIS_DIM = HEAD // 2
IN_COLS = 5 * HGRN_WIDTH + ATT_WIDTH + 2 * KV_WIDTH
COL_QH, COL_FF, COL_FB, COL_IH, COL_GH = (i * HGRN_WIDTH for i in range(5))
COL_QA = 5 * HGRN_WIDTH
COL_KA = COL_QA + ATT_WIDTH
COL_VA = COL_KA + KV_WIDTH
N_GROUPS = 4
EXPERTS_PER_GROUP = 8
N_EXPERTS = N_GROUPS * EXPERTS_PER_GROUP
TOP_K = 2
D_EXPERT = 1024
EPS = 1e-6

LANES = 128
SUBLANES = 8
CHUNK = 128
HGRN_UNROLL = 4
GROUP_LOG_BOUND = 10.0
VMEM_LIMIT = 56 * 1024 * 1024
MOE_TILE = 256
ATTN_Q_BLOCK = 1024
ATTN_CHAIN_ROWS = 512
MOE_COL_BLOCK = 256
DMA_ISSUE_UNROLL = 32

_NT = (((1,), (1,)), ((), ()))


def _cparams(sem):
    return pltpu.CompilerParams(dimension_semantics=sem, vmem_limit_bytes=VMEM_LIMIT)


def _rms_rows(x, w):
    ms = jnp.mean(x * x, axis=-1, keepdims=True)
    return x * lax.rsqrt(ms + EPS) * w


def _silu(x):
    return x * jax.nn.sigmoid(x)


def _row_tile(t, pref):
    while t % pref:
        pref //= 2
    return pref


def _for_row_chunks(n_rows, rc, fn):
    def body(c, carry):
        fn(pl.ds(pl.multiple_of(c * rc, rc), rc))
        return carry
    lax.fori_loop(0, n_rows // rc, body, 0)


def _norm_inproj_body(x_ref, nw_ref, w_ref, o_ref, xn_ref, *, tm, rc):
    @pl.when(pl.program_id(1) == 0)
    def _():
        def one(r):
            xn_ref[r, :] = _rms_rows(x_ref[r, :], nw_ref[...]).astype(BF16)
        _for_row_chunks(tm, rc, one)

    o_ref[...] = jnp.dot(xn_ref[...], w_ref[...].astype(BF16), preferred_element_type=F32)


def _norm_inproj(x, norm_w, w_in, layer):
    t, d = x.shape
    n = w_in.shape[-1]
    tm = _row_tile(t, 1024)
    tn = 512
    return pl.pallas_call(
        functools.partial(_norm_inproj_body, tm=tm, rc=min(256, tm)),
        grid=(t // tm, n // tn),
        in_specs=[
            pl.BlockSpec((tm, d), lambda i, j: (i, 0)),
            pl.BlockSpec((None, 1, d), lambda i, j: (layer, 0, 0)),
            pl.BlockSpec((None, d, tn), lambda i, j: (layer, 0, j)),
        ],
        out_specs=pl.BlockSpec((tm, tn), lambda i, j: (i, j)),
        out_shape=jax.ShapeDtypeStruct((t, n), F32),
        scratch_shapes=[pltpu.VMEM((tm, d), BF16)],
        compiler_params=_cparams(("parallel", "arbitrary")),
        name="norm_inproj",
    )(x, norm_w.reshape(-1, 1, d), w_in)


def _qkprep_body(pos_ref, q_ref, k_ref, v_ref, c_ref, s_ref, qw_ref, kw_ref, qo_ref, ko_ref, vo_ref):
    del pos_ref
    cos = c_ref[...]
    sin = s_ref[...]
    lane = lax.broadcasted_iota(jnp.int32, cos.shape, 1)
    first_half = (lane & (ROPE_AXIS_DIM // 2)) == 0
    scale = float(HEAD ** -0.5 * np.log2(np.e))

    def rope(x):
        partner = jnp.where(first_half, pltpu.roll(x, LANES - ROPE_AXIS_DIM // 2, 1),
                            pltpu.roll(x, ROPE_AXIS_DIM // 2, 1))
        return x * cos + partner * sin

    for h in range(N_Q_HEADS):
        sl = slice(h * HEAD, (h + 1) * HEAD)
        qo_ref[:, sl] = (rope(_rms_rows(q_ref[:, sl], qw_ref[...])) * scale).astype(BF16)
    for h in range(N_KV_HEADS):
        sl = slice(h * HEAD, (h + 1) * HEAD)
        ko_ref[:, sl] = rope(_rms_rows(k_ref[:, sl], kw_ref[...])).astype(BF16)
    vo_ref[...] = v_ref[...].astype(BF16)


def _rope_tables(max_len):
    pos = jnp.arange(max_len, dtype=jnp.int32)
    row = (pos // GRID_W).astype(F32)
    col = (pos % GRID_W).astype(F32)
    inv_freq = ROPE_THETA ** (-jnp.arange(0, ROPE_AXIS_DIM, 2, dtype=F32) / ROPE_AXIS_DIM)
    ang_r = row[:, None] * inv_freq
    ang_c = col[:, None] * inv_freq
    cos = jnp.concatenate([jnp.cos(ang_r), jnp.cos(ang_r), jnp.cos(ang_c), jnp.cos(ang_c)], axis=-1)
    sin = jnp.concatenate([-jnp.sin(ang_r), jnp.sin(ang_r), -jnp.sin(ang_c), jnp.sin(ang_c)], axis=-1)
    return cos, sin


def _qkprep(proj, cos, sin, q_norm_w, k_norm_w, layer, seq_lens):
    t = proj.shape[0]
    tp = min(256, min(seq_lens))
    pos_blk = np.concatenate([np.arange(l // tp) for l in seq_lens]).astype(np.int32)
    grid_spec = pltpu.PrefetchScalarGridSpec(
        num_scalar_prefetch=1,
        grid=(t // tp,),
        in_specs=[
            pl.BlockSpec((tp, ATT_WIDTH), lambda i, p: (i, COL_QA // ATT_WIDTH)),
            pl.BlockSpec((tp, KV_WIDTH), lambda i, p: (i, COL_KA // KV_WIDTH)),
            pl.BlockSpec((tp, KV_WIDTH), lambda i, p: (i, COL_VA // KV_WIDTH)),
            pl.BlockSpec((tp, HEAD), lambda i, p: (p[i], 0)),
            pl.BlockSpec((tp, HEAD), lambda i, p: (p[i], 0)),
            pl.BlockSpec((None, 1, HEAD), lambda i, p: (layer, 0, 0)),
            pl.BlockSpec((None, 1, HEAD), lambda i, p: (layer, 0, 0)),
        ],
        out_specs=[
            pl.BlockSpec((tp, ATT_WIDTH), lambda i, p: (i, 0)),
            pl.BlockSpec((tp, KV_WIDTH), lambda i, p: (i, 0)),
            pl.BlockSpec((tp, KV_WIDTH), lambda i, p: (i, 0)),
        ],
    )
    return pl.pallas_call(
        _qkprep_body,
        grid_spec=grid_spec,
        out_shape=[jax.ShapeDtypeStruct((t, ATT_WIDTH), BF16),
                   jax.ShapeDtypeStruct((t, KV_WIDTH), BF16),
                   jax.ShapeDtypeStruct((t, KV_WIDTH), BF16)],
        compiler_params=_cparams(("parallel",)),
        name="qkprep",
    )(jnp.asarray(pos_blk), proj, proj, proj, cos, sin,
      q_norm_w.reshape(-1, 1, HEAD), k_norm_w.reshape(-1, 1, HEAD))


def _attn_body(qt_ref, kt_ref, first_ref, last_ref, q_ref, k_ref, v_ref, o_ref, m_ref, acc_ref, *, tq):
    del qt_ref, kt_ref
    it = pl.program_id(1)

    @pl.when(first_ref[it] == 1)
    def _():
        m_ref[...] = jnp.full_like(m_ref, -jnp.inf)
        acc_ref[...] = jnp.zeros_like(acc_ref)

    k = k_ref[...]
    lane = lax.broadcasted_iota(jnp.int32, k.shape, 1)
    v1 = jnp.concatenate([v_ref[...], jnp.where(lane == 0, 1.0, 0.0).astype(BF16)], axis=1)
    rc = min(tq, ATTN_CHAIN_ROWS)
    for h, rb in [(h, rb) for h in range(KV_GROUP) for rb in range(tq // rc)]:
        rows = slice(h * tq + rb * rc, h * tq + (rb + 1) * rc)
        q = q_ref[rb * rc:(rb + 1) * rc, h * HEAD:(h + 1) * HEAD]
        s = lax.dot_general(q, k, _NT, preferred_element_type=F32)
        m_prev = m_ref[rows, :]
        m_new = jnp.maximum(m_prev, jnp.max(s, axis=-1, keepdims=True))
        alpha = jnp.exp2(m_prev - m_new)
        p = jnp.exp2(s - m_new[:, 0:1]).astype(BF16)
        pv = jnp.dot(p, v1, preferred_element_type=F32)
        acc = acc_ref[rows, :]
        acc_ref[rows, :] = jnp.concatenate([alpha * acc[:, :HEAD], alpha * acc[:, HEAD:]], axis=1) + pv
        m_ref[rows, :] = m_new

    @pl.when(last_ref[it] == 1)
    def _():
        for h in range(KV_GROUP):
            acc = acc_ref[h * tq:(h + 1) * tq, :]
            o_ref[:, h * HEAD:(h + 1) * HEAD] = acc[:, :HEAD] / acc[:, HEAD:HEAD + 1]


def _attention(qh, kh, vh, seq_lens):
    t = qh.shape[0]
    tq = min(ATTN_Q_BLOCK, min(seq_lens))
    tk = min(1024, min(seq_lens))
    qt, kt, first, last = [], [], [], []
    start = 0
    for l in seq_lens:
        nk = l // tk
        for qb in range(l // tq):
            for kb in range(nk):
                qt.append(start // tq + qb)
                kt.append(start // tk + kb)
                first.append(int(kb == 0))
                last.append(int(kb == nk - 1))
        start += l
    tabs = [jnp.asarray(np.asarray(a, np.int32)) for a in (qt, kt, first, last)]
    qw = KV_GROUP * HEAD
    grid_spec = pltpu.PrefetchScalarGridSpec(
        num_scalar_prefetch=4,
        grid=(N_KV_HEADS, len(qt)),
        in_specs=[
            pl.BlockSpec((tq, qw), lambda g, i, qt_, kt_, f_, l_: (qt_[i], g)),
            pl.BlockSpec((tk, HEAD), lambda g, i, qt_, kt_, f_, l_: (kt_[i], g)),
            pl.BlockSpec((tk, HEAD), lambda g, i, qt_, kt_, f_, l_: (kt_[i], g)),
        ],
        out_specs=pl.BlockSpec((tq, qw), lambda g, i, qt_, kt_, f_, l_: (qt_[i], g)),
        scratch_shapes=[pltpu.VMEM((KV_GROUP * tq, LANES), F32), pltpu.VMEM((KV_GROUP * tq, 2 * HEAD), F32)],
    )
    return pl.pallas_call(
        functools.partial(_attn_body, tq=tq),
        grid_spec=grid_spec,
        out_shape=jax.ShapeDtypeStruct((t, ATT_WIDTH), F32),
        compiler_params=_cparams(("parallel", "arbitrary")),
        name="attention",
    )(*tabs, qh, kh, vh)


_LEVEL_HALVES = (8, 16, 32, 64)


def _level_ref_groups(rev, hs):
    g = np.arange(CHUNK // SUBLANES)
    per = 2 * hs // SUBLANES
    return [int(x) for x in (g // per) * per + (hs // SUBLANES if rev else hs // SUBLANES - 1)]


def _hgrn_constants(rev):
    i = np.arange(CHUNK)
    cum = (i[None, :] >= i[:, None]) if rev else (i[None, :] <= i[:, None])
    wsel = (np.arange(SUBLANES * CHUNK)[:, None] // CHUNK) == (np.arange(CHUNK)[None, :] % SUBLANES)
    ii, jj = i[:, None], i[None, :]
    lvl = np.full((CHUNK, CHUNK), 5, np.int32)
    for n, hs in reversed(list(enumerate(_LEVEL_HALVES, start=1))):
        lvl[(ii // (2 * hs)) == (jj // (2 * hs))] = n
    lvl[(ii // SUBLANES) == (jj // SUBLANES)] = 0
    lvl[(jj < ii) if rev else (jj > ii)] = 5
    return (jnp.asarray(cum.astype(np.float32), BF16), jnp.asarray(wsel.astype(np.float32), BF16),
            jnp.asarray(lvl))


def _hgrn_chunk(rev, mild, q_raw, f_raw, v, lbv, cum_ref, wsel_ref, lvl, st_ref):
    groups = CHUNK // SUBLANES
    edge = 0 if rev else CHUNK - 1
    n_lvl = len(_LEVEL_HALVES)
    q = _silu(q_raw)
    f = lbv + (1.0 - lbv) * jax.nn.sigmoid(f_raw)
    k = 1.0 - f
    lf = jnp.log(f)
    hi = lf.astype(BF16)
    r1 = lf - hi.astype(F32)
    mid = r1.astype(BF16)
    lo = (r1 - mid.astype(F32)).astype(BF16)
    cs = jnp.dot(cum_ref[...], jnp.concatenate([hi, mid, lo], axis=1), preferred_element_type=F32)
    b = cs[:, :HEAD] + cs[:, HEAD:2 * HEAD] + cs[:, 2 * HEAD:]
    b3 = b.reshape(groups, SUBLANES, HEAD)
    g_first = jnp.broadcast_to(b3[:, 0:1, :], b3.shape)
    g_last = jnp.broadcast_to(b3[:, SUBLANES - 1:SUBLANES, :], b3.shape)
    near, far = (g_last, g_first) if rev else (g_first, g_last)

    level_scores = []
    for hs in _LEVEL_HALVES:
        b_ref = jnp.concatenate([far[g:g + 1] for g in _level_ref_groups(rev, hs)], axis=0)
        a = jnp.exp(-jnp.abs(b - b_ref.reshape(CHUNK, HEAD)))
        level_scores.append(lax.dot_general((q * a).astype(BF16), (k * a).astype(BF16), _NT,
                                            preferred_element_type=F32))

    def group_scores_factored():
        d = b - near.reshape(CHUNK, HEAD)
        return lax.dot_general((q * jnp.exp(d)).astype(BF16), (k * jnp.exp(-d)).astype(BF16), _NT,
                               preferred_element_type=F32)

    def group_scores_pairwise():
        sub = lax.broadcasted_iota(jnp.int32, (groups, SUBLANES, HEAD), 1)
        q3 = q.reshape(groups, SUBLANES, HEAD)
        k3 = k.reshape(groups, SUBLANES, HEAD)
        slabs = []
        for j in range(SUBLANES):
            bj = jnp.broadcast_to(b3[:, j:j + 1, :], b3.shape)
            kj = jnp.broadcast_to(k3[:, j:j + 1, :], k3.shape)
            ok = (sub <= j) if rev else (sub >= j)
            e = jnp.exp(jnp.where(ok, b3 - bj, -jnp.inf))
            slabs.append((q3 * kj * e).reshape(CHUNK, HEAD).astype(BF16))
        return jnp.dot(jnp.concatenate(slabs, axis=1), wsel_ref[...], preferred_element_type=F32)

    scores = group_scores_factored() if mild else group_scores_pairwise()
    scores = jnp.where(lvl == 0, scores, 0.0)
    for n, s in enumerate(level_scores, start=1):
        scores = jnp.where(lvl == n, s, scores)
    o_intra = jnp.dot(scores.astype(BF16), v.astype(BF16), preferred_element_type=F32)

    st = st_ref[...]
    b_edge = b[edge:edge + 1, :]
    o_inter = lax.dot_general((q * jnp.exp(b)).astype(BF16), st.astype(BF16), _NT, preferred_element_type=F32)
    k_end = (k * jnp.exp(b_edge - b)).astype(BF16)
    st_ref[...] = st * jnp.exp(b_edge) + jnp.dot(v.T.astype(BF16), k_end, preferred_element_type=F32)
    return o_intra + o_inter


def _hgrn_body(rsf_ref, rsb_ref, qf_ref, ff_ref, vf_ref, qb_ref, fb_ref, vb_ref, lbf_ref, lbb_ref,
               cumf_ref, cumb_ref, wsel_ref, lvlf_ref, lvlb_ref, of_ref, ob_ref, stf_ref, stb_ref, *, nchunk, nblk):
    it = pl.program_id(1)

    @pl.when(rsf_ref[it] == 1)
    def _():
        stf_ref[...] = jnp.zeros_like(stf_ref)

    @pl.when(rsb_ref[nblk - 1 - it] == 1)
    def _():
        stb_ref[...] = jnp.zeros_like(stb_ref)

    lbf = lbf_ref[...]
    lbb = lbb_ref[...]
    lvlf = lvlf_ref[...]
    lvlb = lvlb_ref[...]

    def run(mild):
        def chunk(ci, carry):
            rf = pl.ds(pl.multiple_of(ci * CHUNK, CHUNK), CHUNK)
            rb = pl.ds(pl.multiple_of((nchunk - 1 - ci) * CHUNK, CHUNK), CHUNK)
            of_ref[rf, :] = _hgrn_chunk(False, mild, qf_ref[rf, :], ff_ref[rf, :], vf_ref[rf, :], lbf, cumf_ref,
                                        wsel_ref, lvlf, stf_ref)
            ob_ref[rb, :] = _hgrn_chunk(True, mild, qb_ref[rb, :], fb_ref[rb, :], vb_ref[rb, :], lbb, cumb_ref,
                                        wsel_ref, lvlb, stb_ref)
            return carry
        lax.fori_loop(0, nchunk, chunk, 0, unroll=HGRN_UNROLL if mild else 1)

    mild = jnp.minimum(jnp.min(ff_ref[...]), jnp.min(fb_ref[...])) >= -(GROUP_LOG_BOUND - 1.0)

    @pl.when(mild)
    def _():
        run(True)

    @pl.when(jnp.logical_not(mild))
    def _():
        run(False)


def _hgrn(proj, lb_f, lb_b, seq_lens):
    t = proj.shape[0]
    lblk = min(512, min(seq_lens))
    nblk = t // lblk
    starts = np.cumsum([0] + list(seq_lens))
    reset_f = np.zeros((nblk,), np.int32)
    reset_b = np.zeros((nblk,), np.int32)
    for s, l in zip(starts[:-1], seq_lens):
        reset_f[s // lblk] = 1
        reset_b[(s + l) // lblk - 1] = 1
    cum_f, wsel, lvl_f = _hgrn_constants(False)
    cum_b, _, lvl_b = _hgrn_constants(True)

    def col(off, rev):
        return lambda h, i, rf, rb: (nblk - 1 - i if rev else i, off // HEAD + h)

    const = lambda h, i, rf, rb: (0, 0)
    head = lambda h, i, rf, rb: (h, 0, 0)
    grid_spec = pltpu.PrefetchScalarGridSpec(
        num_scalar_prefetch=2,
        grid=(N_HGRN_HEADS, nblk),
        in_specs=[
            pl.BlockSpec((lblk, HEAD), col(COL_QH, False)),
            pl.BlockSpec((lblk, HEAD), col(COL_FF, False)),
            pl.BlockSpec((lblk, HEAD), col(COL_IH, False)),
            pl.BlockSpec((lblk, HEAD), col(COL_QH, True)),
            pl.BlockSpec((lblk, HEAD), col(COL_FB, True)),
            pl.BlockSpec((lblk, HEAD), col(COL_IH, True)),
            pl.BlockSpec((None, 1, HEAD), head),
            pl.BlockSpec((None, 1, HEAD), head),
            pl.BlockSpec(cum_f.shape, const),
            pl.BlockSpec(cum_b.shape, const),
            pl.BlockSpec(wsel.shape, const),
            pl.BlockSpec(lvl_f.shape, const),
            pl.BlockSpec(lvl_b.shape, const),
        ],
        out_specs=[pl.BlockSpec((lblk, HEAD), lambda h, i, rf, rb: (i, h)),
                   pl.BlockSpec((lblk, HEAD), lambda h, i, rf, rb: (nblk - 1 - i, h))],
        scratch_shapes=[pltpu.VMEM((HEAD, HEAD), F32), pltpu.VMEM((HEAD, HEAD), F32)],
    )
    return pl.pallas_call(
        functools.partial(_hgrn_body, nchunk=lblk // CHUNK, nblk=nblk),
        grid_spec=grid_spec,
        out_shape=[jax.ShapeDtypeStruct((t, HGRN_WIDTH), F32), jax.ShapeDtypeStruct((t, HGRN_WIDTH), F32)],
        compiler_params=_cparams(("parallel", "arbitrary")),
        name="hgrn",
    )(jnp.asarray(reset_f), jnp.asarray(reset_b), proj, proj, proj, proj, proj, proj, lb_f, lb_b,
      cum_f, cum_b, wsel, lvl_f, lvl_b)


def _merge_outproj_body(of_ref, ob_ref, g_ref, a_ref, x_ref, hw_ref, aw_ref, w_ref, o_ref, mg_ref, *, tm, rc):
    @pl.when(pl.program_id(1) == 0)
    def _():
        def one(r):
            for h in range(N_HGRN_HEADS):
                sl = slice(h * HEAD, (h + 1) * HEAD)
                o = of_ref[r, sl] + ob_ref[r, sl]
                mg_ref[r, sl] = (_rms_rows(o, hw_ref[...]) * _silu(g_ref[r, sl])).astype(BF16)
            mg_ref[r, HGRN_WIDTH:] = _rms_rows(a_ref[r, :], aw_ref[...]).astype(BF16)
        _for_row_chunks(tm, rc, one)

    o_ref[...] = x_ref[...] + jnp.dot(mg_ref[...], w_ref[...].astype(BF16), preferred_element_type=F32)


def _merge_outproj(o_f, o_b, proj, attn, x, hgrn_norm_w, attn_norm_w, w_out, layer):
    t, d = x.shape
    tm = _row_tile(t, 1024)
    tn = 512
    return pl.pallas_call(
        functools.partial(_merge_outproj_body, tm=tm, rc=min(256, tm)),
        grid=(t // tm, d // tn),
        in_specs=[
            pl.BlockSpec((tm, HGRN_WIDTH), lambda i, j: (i, 0)),
            pl.BlockSpec((tm, HGRN_WIDTH), lambda i, j: (i, 0)),
            pl.BlockSpec((tm, HGRN_WIDTH), lambda i, j: (i, COL_GH // HGRN_WIDTH)),
            pl.BlockSpec((tm, ATT_WIDTH), lambda i, j: (i, 0)),
            pl.BlockSpec((tm, tn), lambda i, j: (i, j)),
            pl.BlockSpec((None, 1, HEAD), lambda i, j: (layer, 0, 0)),
            pl.BlockSpec((None, 1, ATT_WIDTH), lambda i, j: (layer, 0, 0)),
            pl.BlockSpec((None, d, tn), lambda i, j: (layer, 0, j)),
        ],
        out_specs=pl.BlockSpec((tm, tn), lambda i, j: (i, j)),
        out_shape=jax.ShapeDtypeStruct((t, d), F32),
        scratch_shapes=[pltpu.VMEM((tm, d), BF16)],
        compiler_params=_cparams(("parallel", "arbitrary")),
        name="merge_outproj",
    )(o_f, o_b, proj, attn, x, hgrn_norm_w.reshape(-1, 1, HEAD), attn_norm_w.reshape(-1, 1, ATT_WIDTH), w_out)


def _router_body(x_ref, nw_ref, w_ref, b_ref, o_ref, cnt_ref, rows_ref):
    @pl.when(pl.program_id(0) == 0)
    def _():
        cnt_ref[...] = jnp.zeros_like(cnt_ref)

    tm, d = x_ref.shape
    for j in range(d // LANES):
        rows_ref[pl.ds(j, tm, stride=d // LANES), :] = x_ref[:, j * LANES:(j + 1) * LANES]

    xn = _rms_rows(x_ref[...], nw_ref[...])
    w = w_ref[...]
    xh = xn.astype(BF16)
    xl = (xn - xh.astype(F32)).astype(BF16)
    wh = w.astype(BF16)
    wl = (w - wh.astype(F32)).astype(BF16)
    logits = (jnp.dot(xh, wh, preferred_element_type=F32) + jnp.dot(xl, wh, preferred_element_type=F32)
              + jnp.dot(xh, wl, preferred_element_type=F32)) + b_ref[...]
    lane = lax.broadcasted_iota(jnp.int32, logits.shape, 1).astype(F32)
    neg = -jnp.inf
    big = float(LANES)

    def first_argmax(vals):
        top = jnp.max(vals, axis=-1, keepdims=True)
        idx = jnp.min(jnp.where(vals == top, lane, big), axis=-1, keepdims=True)
        return top, idx

    gl = jnp.where(lane < N_GROUPS, logits, neg)
    gmax, gsel = first_argmax(gl)
    g_w = 1.0 / jnp.sum(jnp.exp(gl - gmax), axis=-1, keepdims=True)
    lo = N_GROUPS + EXPERTS_PER_GROUP * gsel
    el = jnp.where((lane >= lo) & (lane < lo + EXPERTS_PER_GROUP), logits, neg)
    t1, i1 = first_argmax(el)
    t2, i2 = first_argmax(jnp.where(lane == i1, neg, el))
    e = jnp.exp(t2 - t1)
    w1 = g_w / (1.0 + e)
    w2 = g_w * e / (1.0 + e)
    e1 = i1 - N_GROUPS
    e2 = i2 - N_GROUPS
    o_ref[...] = jnp.where(lane == 0, e1, jnp.where(lane == 1, e2, jnp.where(lane == 2, w1, jnp.where(lane == 3, w2, 0.0))))
    hits = jnp.where(lane == e1, 1.0, 0.0) + jnp.where(lane == e2, 1.0, 0.0)
    cnt_ref[...] += jnp.sum(hits, axis=0, keepdims=True)


def _router(h, ffn_norm_w, w_group, b_group, w_router, b_router, layer):
    t, d = h.shape
    tm = _row_tile(t, 256)
    per_row = d // LANES
    pad = LANES - N_GROUPS - N_EXPERTS
    w = jnp.concatenate([w_group[layer], w_router[layer], jnp.zeros((d, pad), F32)], axis=1)
    b = jnp.concatenate([b_group[layer], b_router[layer], jnp.zeros((pad,), F32)])[None, :]
    return pl.pallas_call(
        _router_body,
        grid=(t // tm,),
        in_specs=[
            pl.BlockSpec((tm, d), lambda i: (i, 0)),
            pl.BlockSpec((None, 1, d), lambda i: (layer, 0, 0)),
            pl.BlockSpec((d, LANES), lambda i: (0, 0)),
            pl.BlockSpec((1, LANES), lambda i: (0, 0)),
        ],
        out_specs=[pl.BlockSpec((tm, LANES), lambda i: (i, 0)), pl.BlockSpec((1, LANES), lambda i: (0, 0)),
                   pl.BlockSpec((tm * per_row, LANES), lambda i: (i, 0))],
        out_shape=[jax.ShapeDtypeStruct((t, LANES), F32), jax.ShapeDtypeStruct((1, LANES), F32),
                   jax.ShapeDtypeStruct((t * per_row, LANES), F32)],
        compiler_params=_cparams(("arbitrary",)),
        name="router",
    )(h, ffn_norm_w.reshape(-1, 1, d), w, b)


def _dispatch_tables(route, counts, t):
    tm = MOE_TILE
    a = t * TOP_K
    n_tiles = a // tm + N_EXPERTS
    eid = route[:, :TOP_K].astype(jnp.int32).reshape(a)
    wts = route[:, TOP_K:2 * TOP_K].reshape(a)
    order = jnp.argsort(eid, stable=True).astype(jnp.int32)
    counts = counts[0, :N_EXPERTS].astype(jnp.int32)
    seg_start = jnp.cumsum(counts) - counts
    padded = (counts + tm - 1) // tm * tm
    pad_end = jnp.cumsum(padded)
    pad_start = pad_end - padded
    n_valid = (pad_end[-1] // tm).astype(jnp.int32)
    tile = jnp.arange(n_tiles, dtype=jnp.int32)
    tile_expert = jnp.sum((pad_end[None, :] <= (tile * tm)[:, None]).astype(jnp.int32), axis=1)
    tile_expert = jnp.minimum(tile_expert, N_EXPERTS - 1)
    tile_expert = jnp.where(tile < n_valid, tile_expert, tile_expert[jnp.maximum(n_valid - 1, 0)])
    first_row = tile * tm - pad_start[tile_expert]
    n_rows = jnp.where(tile < n_valid, jnp.clip(counts[tile_expert] - first_row, 0, tm), 0).astype(jnp.int32)
    within = jnp.arange(tm, dtype=jnp.int32)[None, :]
    valid = within < n_rows[:, None]
    src = order[jnp.clip((seg_start[tile_expert] + first_row)[:, None] + within, 0, a - 1)]
    row_tok = jnp.where(valid, src // TOP_K, 0).reshape(n_tiles, 1, tm)
    row_w = jnp.where(valid, wts[src], 0.0).reshape(n_tiles * tm, 1)
    row_dst = jnp.where(valid, (src % TOP_K) * t + src // TOP_K, TOP_K * t + within).reshape(n_tiles, 1, tm)
    return tile_expert, n_valid.reshape(1), row_tok, row_w, row_dst


def _cast_rows(src_ref, dst_ref, rc):
    def one(r):
        dst_ref[r, :] = src_ref[r, :].astype(dst_ref.dtype)
    _for_row_chunks(src_ref.shape[0], rc, one)


def _expert_changed(te_ref, i):
    return (i == 0) | (te_ref[i] != te_ref[jnp.maximum(i - 1, 0)])


def _gateup_body(te_ref, nv_ref, rt_ref, rtn_ref, x_hbm, nw_ref, wg_ref, wu_ref, o_ref, xbuf, xn_ref, wgb, wub, sem,
                 *, tm):
    i = pl.program_id(0)
    nv = nv_ref[0]
    slot = i % 2

    d = xn_ref.shape[1]
    per_row = d // LANES

    def row_copy(idx_ref, sl, r):
        src = pl.multiple_of(idx_ref[0, r] * per_row, per_row)
        dst = r * per_row if isinstance(r, int) else pl.multiple_of(r * per_row, per_row)
        return pltpu.make_async_copy(x_hbm.at[pl.ds(src, per_row), :], xbuf.at[sl, pl.ds(dst, per_row), :], sem.at[sl])

    def wait_slot(sl):
        pltpu.make_async_copy(x_hbm.at[pl.ds(0, tm * per_row), :], xbuf.at[sl], sem.at[sl]).wait()

    @pl.when((i == 0) & (nv > 0))
    def _():
        def body(r, carry):
            row_copy(rt_ref, 0, r).start()
            return carry
        lax.fori_loop(0, tm, body, 0, unroll=DMA_ISSUE_UNROLL)

    @pl.when(i < nv)
    def _():
        wait_slot(slot)

        @pl.when(_expert_changed(te_ref, i))
        def _():
            _cast_rows(wg_ref, wgb, 256)
            _cast_rows(wu_ref, wub, 256)

        xs = xbuf.at[slot]
        ss = jnp.zeros((tm, LANES), F32)
        for j in range(per_row):
            xj = xs[pl.ds(j, tm, stride=per_row), :]
            ss = ss + xj * xj
        inv = lax.rsqrt(jnp.sum(ss, axis=-1, keepdims=True) * (1.0 / d) + EPS)
        for j in range(per_row):
            cols = slice(j * LANES, (j + 1) * LANES)
            xn_ref[:, cols] = (xs[pl.ds(j, tm, stride=per_row), :] * inv * nw_ref[:, cols]).astype(BF16)
        de = o_ref.shape[1]
        nblk = de // MOE_COL_BLOCK
        per = tm // nblk
        for j in range(nblk):
            for r in range(j * per, (j + 1) * per):
                row_copy(rtn_ref, 1 - slot, r).start()
            cols = slice(j * MOE_COL_BLOCK, (j + 1) * MOE_COL_BLOCK)
            g = jnp.dot(xn_ref[...], wgb[:, cols], preferred_element_type=F32)
            u = jnp.dot(xn_ref[...], wub[:, cols], preferred_element_type=F32)
            o_ref[:, cols] = (_silu(g) * u).astype(BF16)

        @pl.when(i == nv - 1)
        def _():
            wait_slot(1 - slot)

    @pl.when(i >= nv)
    def _():
        o_ref[...] = jnp.zeros_like(o_ref)


def _expert_gateup(h_rows, ffn_norm_w, w_gate, w_up, tile_expert, n_valid, row_tok, layer):
    d = w_gate.shape[-2]
    tm = MOE_TILE
    n_tiles = row_tok.shape[0]
    de = w_gate.shape[-1]
    smem_rows = lambda f: pl.BlockSpec((None, 1, tm), f, memory_space=pltpu.SMEM)
    wspec = pl.BlockSpec((None, None, d, de), lambda i, te, nv: (layer, te[i], 0, 0))
    grid_spec = pltpu.PrefetchScalarGridSpec(
        num_scalar_prefetch=2,
        grid=(n_tiles,),
        in_specs=[
            smem_rows(lambda i, te, nv: (i, 0, 0)),
            smem_rows(lambda i, te, nv: (jnp.minimum(i + 1, n_tiles - 1), 0, 0)),
            pl.BlockSpec(memory_space=pl.ANY),
            pl.BlockSpec((None, 1, d), lambda i, te, nv: (layer, 0, 0)),
            wspec,
            wspec,
        ],
        out_specs=pl.BlockSpec((tm, de), lambda i, te, nv: (i, 0)),
        scratch_shapes=[pltpu.VMEM((2, tm * (d // LANES), LANES), F32), pltpu.VMEM((tm, d), BF16),
                        pltpu.VMEM((d, de), BF16), pltpu.VMEM((d, de), BF16), pltpu.SemaphoreType.DMA((2,))],
    )
    return pl.pallas_call(
        functools.partial(_gateup_body, tm=tm),
        grid_spec=grid_spec,
        out_shape=jax.ShapeDtypeStruct((n_tiles * tm, de), BF16),
        compiler_params=_cparams(("arbitrary",)),
        name="expert_gateup",
    )(tile_expert, n_valid, row_tok, row_tok, h_rows, ffn_norm_w.reshape(-1, 1, d), w_gate, w_up)


def _down_body(te_ref, nv_ref, rd_ref, rdp_ref, h_ref, rw_ref, wd_ref, y_hbm, ybuf, wdb, sem, *, tm, spare_row0):
    i = pl.program_id(0)
    nv = nv_ref[0]
    slot = i % 2

    def row_copy(idx_ref, sl, r):
        return pltpu.make_async_copy(ybuf.at[sl, pl.ds(r, 1), :], y_hbm.at[pl.ds(idx_ref[0, r], 1), :], sem.at[sl])

    def wait_slot(sl):
        pltpu.make_async_copy(ybuf.at[sl], y_hbm.at[pl.ds(0, tm), :], sem.at[sl]).wait()

    @pl.when(i == 0)
    def _():
        ybuf[...] = jnp.zeros_like(ybuf)
        init = pltpu.make_async_copy(ybuf.at[0], y_hbm.at[pl.ds(spare_row0, tm), :], sem.at[0])
        init.start()
        init.wait()

    @pl.when(i < nv)
    def _():
        @pl.when(i >= 1)
        def _():
            wait_slot(slot)

        @pl.when(_expert_changed(te_ref, i))
        def _():
            _cast_rows(wd_ref, wdb, 256)

        d = ybuf.shape[2]
        nblk = d // MOE_COL_BLOCK
        per = tm // nblk
        for j in range(nblk):
            for r in range(j * per, (j + 1) * per):
                row_copy(rdp_ref, 1 - slot, r).start()
            cols = slice(j * MOE_COL_BLOCK, (j + 1) * MOE_COL_BLOCK)
            ybuf[slot, :, cols] = jnp.dot(h_ref[...], wdb[:, cols], preferred_element_type=F32) * rw_ref[...]

        @pl.when(i == nv - 1)
        def _():
            wait_slot(1 - slot)

            def body(r, carry):
                row_copy(rd_ref, slot, r).start()
                return carry
            lax.fori_loop(0, tm, body, 0, unroll=DMA_ISSUE_UNROLL)
            wait_slot(slot)


def _expert_down(hid, w_down, tile_expert, n_valid, row_dst, row_w, t, layer):
    tm = MOE_TILE
    n_tiles = row_dst.shape[0]
    de, d = w_down.shape[-2:]
    spare = TOP_K * t + jnp.arange(tm, dtype=jnp.int32).reshape(1, 1, tm)
    row_dst_prev = jnp.concatenate([spare, row_dst[:-1]], axis=0)
    smem_rows = pl.BlockSpec((None, 1, tm), lambda i, te, nv: (i, 0, 0), memory_space=pltpu.SMEM)
    grid_spec = pltpu.PrefetchScalarGridSpec(
        num_scalar_prefetch=2,
        grid=(n_tiles,),
        in_specs=[
            smem_rows,
            smem_rows,
            pl.BlockSpec((tm, de), lambda i, te, nv: (i, 0)),
            pl.BlockSpec((tm, 1), lambda i, te, nv: (i, 0)),
            pl.BlockSpec((None, None, de, d), lambda i, te, nv: (layer, te[i], 0, 0)),
        ],
        out_specs=pl.BlockSpec(memory_space=pl.ANY),
        scratch_shapes=[pltpu.VMEM((2, tm, d), F32), pltpu.VMEM((de, d), BF16), pltpu.SemaphoreType.DMA((2,))],
    )
    return pl.pallas_call(
        functools.partial(_down_body, tm=tm, spare_row0=TOP_K * t),
        grid_spec=grid_spec,
        out_shape=jax.ShapeDtypeStruct((TOP_K * t + tm, d), F32),
        compiler_params=_cparams(("arbitrary",)),
        name="expert_down",
    )(tile_expert, n_valid, row_dst, row_dst_prev, hid, row_w, w_down)


def _combine_body(h_ref, y0_ref, y1_ref, nw_ref, o_ref, *, final):
    s = h_ref[...] + (y0_ref[...] + y1_ref[...])
    o_ref[...] = _rms_rows(s, nw_ref[...]) if final else s


def _combine(h, y, final_norm_w, final):
    t, d = h.shape
    tm = _row_tile(t, 256)
    nb = t // tm
    return pl.pallas_call(
        functools.partial(_combine_body, final=final),
        grid=(nb,),
        in_specs=[
            pl.BlockSpec((tm, d), lambda i: (i, 0)),
            pl.BlockSpec((tm, d), lambda i: (i, 0)),
            pl.BlockSpec((tm, d), lambda i: (i + nb, 0)),
            pl.BlockSpec((1, d), lambda i: (0, 0)),
        ],
        out_specs=pl.BlockSpec((tm, d), lambda i: (i, 0)),
        out_shape=jax.ShapeDtypeStruct((t, d), F32),
        compiler_params=_cparams(("parallel",)),
        name="combine_final" if final else "combine",
    )(h, y, y, final_norm_w.reshape(1, d))


def _hgrn_lower_bounds(lb_param):
    p = jax.nn.softmax(lb_param.astype(F32), axis=1)
    return jnp.cumsum(p, axis=1) - p[:, :1]


def _trunk(x, seq_lens, mix_norm_w, w_in, hgrn_lb, hgrn_out_norm_w, q_norm_w, k_norm_w, attn_out_norm_w, w_out,
           ffn_norm_w, w_group, b_group, w_router, b_router, w_gate_e, w_up_e, w_down_e, final_norm_w):
    t = x.shape[0]
    depth = w_in.shape[0]
    lb = _hgrn_lower_bounds(hgrn_lb).reshape(2, depth, N_HGRN_HEADS, 1, HEAD)
    cos, sin = _rope_tables(max(seq_lens))
    h = x
    for l in range(depth):
        proj = _norm_inproj(h, mix_norm_w, w_in, l)
        qh, kh, vh = _qkprep(proj, cos, sin, q_norm_w, k_norm_w, l, seq_lens)
        attn = _attention(qh, kh, vh, seq_lens)
        o_f, o_b = _hgrn(proj, lb[0, l], lb[1, l], seq_lens)
        h = _merge_outproj(o_f, o_b, proj, attn, h, hgrn_out_norm_w, attn_out_norm_w, w_out, l)
        route, counts, h_rows = _router(h, ffn_norm_w, w_group, b_group, w_router, b_router, l)
        tile_expert, n_valid, row_tok, row_w, row_dst = _dispatch_tables(route, counts, t)
        hid = _expert_gateup(h_rows, ffn_norm_w, w_gate_e, w_up_e, tile_expert, n_valid, row_tok, l)
        y = _expert_down(hid, w_down_e, tile_expert, n_valid, row_dst, row_w, t, l)
        h = _combine(h, y, final_norm_w, final=(l == depth - 1))
    return h


def kernel(x_prompt, x_sample, mix_norm_w, w_in, hgrn_lb, hgrn_out_norm_w, q_norm_w, k_norm_w, attn_out_norm_w, w_out,
           ffn_norm_w, w_group, b_group, w_router, b_router, w_gate_e, w_up_e, w_down_e, final_norm_w):
    bp, lp, d = x_prompt.shape
    bs, ls, _ = x_sample.shape
    seq_lens = (lp,) * bp + (ls,) * bs
    x = jnp.concatenate([x_prompt.reshape(bp * lp, d), x_sample.reshape(bs * ls, d)], axis=0)
    y = _trunk(x, seq_lens, mix_norm_w, w_in, hgrn_lb, hgrn_out_norm_w, q_norm_w, k_norm_w, attn_out_norm_w, w_out,
               ffn_norm_w, w_group, b_group, w_router, b_router, w_gate_e, w_up_e, w_down_e, final_norm_w)
    return y[:bp * lp].reshape(bp, lp, d), y[bp * lp:].reshape(bs, ls, d)
```

```python
import functools

import numpy as np
import jax
import jax.numpy as jnp
from jax import lax
from jax.experimental import pallas as pl
from jax.experimental.pallas import tpu as pltpu

F32 = jnp.float32
BF16 = jnp.bfloat16

D_MODEL = 2048
DEPTH = 4
N_HGRN_HEADS = 8
HEAD = 128
HGRN_WIDTH = N_HGRN_HEADS * HEAD
N_Q_HEADS = 8
N_KV_HEADS = 2
KV_GROUP = N_Q_HEADS // N_KV_HEADS
ATT_WIDTH = N_Q_HEADS * HEAD
KV_WIDTH = N_KV_HEADS * HEAD
GRID_W = 64
ROPE_THETA = 10000.0
ROPE_AXIS_DIM = HEAD // 2
IN_COLS = 5 * HGRN_WIDTH + ATT_WIDTH + 2 * KV_WIDTH
COL_QH, COL_FF, COL_FB, COL_IH, COL_GH = (i * HGRN_WIDTH for i in range(5))
COL_QA = 5 * HGRN_WIDTH
COL_KA = COL_QA + ATT_WIDTH
COL_VA = COL_KA + KV_WIDTH
N_GROUPS = 4
EXPERTS_PER_GROUP = 8
N_EXPERTS = N_GROUPS * EXPERTS_PER_GROUP
TOP_K = 2
D_EXPERT = 1024
EPS = 1e-6

LANES = 128
SUBLANES = 8
CHUNK = 128
HGRN_UNROLL = 4
GROUP_LOG_BOUND = 10.0
VMEM_LIMIT = 56 * 1024 * 1024
MOE_TILE = 256
ATTN_Q_BLOCK = 2048
ATTN_CHAIN_ROWS = 512
MOE_COL_BLOCK = 256
DMA_ISSUE_UNROLL = 32

_NT = (((1,), (1,)), ((), ()))


def _cparams(sem):
    return pltpu.CompilerParams(dimension_semantics=sem, vmem_limit_bytes=VMEM_LIMIT)


def _rms_rows(x, w):
    ms = jnp.mean(x * x, axis=-1, keepdims=True)
    return x * lax.rsqrt(ms + EPS) * w


def _silu(x):
    return x * jax.nn.sigmoid(x)


def _row_tile(t, pref):
    while t % pref:
        pref //= 2
    return pref


def _for_row_chunks(n_rows, rc, fn):
    def body(c, carry):
        fn(pl.ds(pl.multiple_of(c * rc, rc), rc))
        return carry
    lax.fori_loop(0, n_rows // rc, body, 0)


def _norm_inproj_body(x_ref, nw_ref, w_ref, o_ref, xn_ref, *, tm, rc):
    @pl.when(pl.program_id(1) == 0)
    def _():
        def one(r):
            xn_ref[r, :] = _rms_rows(x_ref[r, :], nw_ref[...]).astype(BF16)
        _for_row_chunks(tm, rc, one)

    o_ref[...] = jnp.dot(xn_ref[...], w_ref[...].astype(BF16), preferred_element_type=F32)


def _norm_inproj(x, norm_w, w_in, layer):
    t, d = x.shape
    n = w_in.shape[-1]
    tm = _row_tile(t, 1024)
    tn = 512
    return pl.pallas_call(
        functools.partial(_norm_inproj_body, tm=tm, rc=min(256, tm)),
        grid=(t // tm, n // tn),
        in_specs=[
            pl.BlockSpec((tm, d), lambda i, j: (i, 0)),
            pl.BlockSpec((None, 1, d), lambda i, j: (layer, 0, 0)),
            pl.BlockSpec((None, d, tn), lambda i, j: (layer, 0, j)),
        ],
        out_specs=pl.BlockSpec((tm, tn), lambda i, j: (i, j)),
        out_shape=jax.ShapeDtypeStruct((t, n), F32),
        scratch_shapes=[pltpu.VMEM((tm, d), BF16)],
        compiler_params=_cparams(("parallel", "arbitrary")),
        name="norm_inproj",
    )(x, norm_w.reshape(-1, 1, d), w_in)


def _qkprep_body(pos_ref, q_ref, k_ref, v_ref, c_ref, s_ref, qw_ref, kw_ref, qo_ref, ko_ref, vo_ref):
    del pos_ref
    cos = c_ref[...]
    sin = s_ref[...]
    lane = lax.broadcasted_iota(jnp.int32, cos.shape, 1)
    first_half = (lane & (ROPE_AXIS_DIM // 2)) == 0
    scale = float(HEAD ** -0.5 * np.log2(np.e))

    def rope(x):
        partner = jnp.where(first_half, pltpu.roll(x, LANES - ROPE_AXIS_DIM // 2, 1),
                            pltpu.roll(x, ROPE_AXIS_DIM // 2, 1))
        return x * cos + partner * sin

    for h in range(N_Q_HEADS):
        sl = slice(h * HEAD, (h + 1) * HEAD)
        qo_ref[:, sl] = (rope(_rms_rows(q_ref[:, sl], qw_ref[...])) * scale).astype(BF16)
    for h in range(N_KV_HEADS):
        sl = slice(h * HEAD, (h + 1) * HEAD)
        ko_ref[:, sl] = rope(_rms_rows(k_ref[:, sl], kw_ref[...])).astype(BF16)
    vo_ref[...] = v_ref[...].astype(BF16)


def _rope_tables(max_len):
    pos = jnp.arange(max_len, dtype=jnp.int32)
    row = (pos // GRID_W).astype(F32)
    col = (pos % GRID_W).astype(F32)
    inv_freq = ROPE_THETA ** (-jnp.arange(0, ROPE_AXIS_DIM, 2, dtype=F32) / ROPE_AXIS_DIM)
    ang_r = row[:, None] * inv_freq
    ang_c = col[:, None] * inv_freq
    cos = jnp.concatenate([jnp.cos(ang_r), jnp.cos(ang_r), jnp.cos(ang_c), jnp.cos(ang_c)], axis=-1)
    sin = jnp.concatenate([-jnp.sin(ang_r), jnp.sin(ang_r), -jnp.sin(ang_c), jnp.sin(ang_c)], axis=-1)
    return cos, sin


def _qkprep(proj, cos, sin, q_norm_w, k_norm_w, layer, seq_lens):
    t = proj.shape[0]
    tp = min(256, min(seq_lens))
    pos_blk = np.concatenate([np.arange(l // tp) for l in seq_lens]).astype(np.int32)
    grid_spec = pltpu.PrefetchScalarGridSpec(
        num_scalar_prefetch=1,
        grid=(t // tp,),
        in_specs=[
            pl.BlockSpec((tp, ATT_WIDTH), lambda i, p: (i, COL_QA // ATT_WIDTH)),
            pl.BlockSpec((tp, KV_WIDTH), lambda i, p: (i, COL_KA // KV_WIDTH)),
            pl.BlockSpec((tp, KV_WIDTH), lambda i, p: (i, COL_VA // KV_WIDTH)),
            pl.BlockSpec((tp, HEAD), lambda i, p: (p[i], 0)),
            pl.BlockSpec((tp, HEAD), lambda i, p: (p[i], 0)),
            pl.BlockSpec((None, 1, HEAD), lambda i, p: (layer, 0, 0)),
            pl.BlockSpec((None, 1, HEAD), lambda i, p: (layer, 0, 0)),
        ],
        out_specs=[
            pl.BlockSpec((tp, ATT_WIDTH), lambda i, p: (i, 0)),
            pl.BlockSpec((tp, KV_WIDTH), lambda i, p: (i, 0)),
            pl.BlockSpec((tp, KV_WIDTH), lambda i, p: (i, 0)),
        ],
    )
    return pl.pallas_call(
        _qkprep_body,
        grid_spec=grid_spec,
        out_shape=[jax.ShapeDtypeStruct((t, ATT_WIDTH), BF16),
                   jax.ShapeDtypeStruct((t, KV_WIDTH), BF16),
                   jax.ShapeDtypeStruct((t, KV_WIDTH), BF16)],
        compiler_params=_cparams(("parallel",)),
        name="qkprep",
    )(jnp.asarray(pos_blk), proj, proj, proj, cos, sin,
      q_norm_w.reshape(-1, 1, HEAD), k_norm_w.reshape(-1, 1, HEAD))


def _attn_body(qt_ref, kt_ref, first_ref, last_ref, q_ref, k_ref, v_ref, o_ref, m_ref, acc_ref, *, tq):
    del qt_ref, kt_ref
    it = pl.program_id(1)

    @pl.when(first_ref[it] == 1)
    def _():
        m_ref[...] = jnp.full_like(m_ref, -jnp.inf)
        acc_ref[...] = jnp.zeros_like(acc_ref)

    k = k_ref[...]
    lane = lax.broadcasted_iota(jnp.int32, k.shape, 1)
    v1 = jnp.concatenate([v_ref[...], jnp.where(lane == 0, 1.0, 0.0).astype(BF16)], axis=1)
    rc = min(tq, ATTN_CHAIN_ROWS)
    for h, rb in [(h, rb) for h in range(KV_GROUP) for rb in range(tq // rc)]:
        rows = slice(h * tq + rb * rc, h * tq + (rb + 1) * rc)
        q = q_ref[rb * rc:(rb + 1) * rc, h * HEAD:(h + 1) * HEAD]
        s = lax.dot_general(q, k, _NT, preferred_element_type=F32)
        m_prev = m_ref[rows, :]
        m_new = jnp.maximum(m_prev, jnp.max(s, axis=-1, keepdims=True))
        alpha = jnp.exp2(m_prev - m_new)
        p = jnp.exp2(s - m_new[:, 0:1]).astype(BF16)
        pv = jnp.dot(p, v1, preferred_element_type=F32)
        acc = acc_ref[rows, :]
        acc_ref[rows, :] = jnp.concatenate([alpha * acc[:, :HEAD], alpha * acc[:, HEAD:]], axis=1) + pv
        m_ref[rows, :] = m_new

    @pl.when(last_ref[it] == 1)
    def _():
        for h in range(KV_GROUP):
            acc = acc_ref[h * tq:(h + 1) * tq, :]
            o_ref[:, h * HEAD:(h + 1) * HEAD] = acc[:, :HEAD] / acc[:, HEAD:HEAD + 1]


def _attention(qh, kh, vh, seq_lens):
    t = qh.shape[0]
    tq = min(ATTN_Q_BLOCK, min(seq_lens))
    tk = min(1024, min(seq_lens))
    qt, kt, first, last = [], [], [], []
    start = 0
    for l in seq_lens:
        nk = l // tk
        for qb in range(l // tq):
            for kb in range(nk):
                qt.append(start // tq + qb)
                kt.append(start // tk + kb)
                first.append(int(kb == 0))
                last.append(int(kb == nk - 1))
        start += l
    tabs = [jnp.asarray(np.asarray(a, np.int32)) for a in (qt, kt, first, last)]
    qw = KV_GROUP * HEAD
    grid_spec = pltpu.PrefetchScalarGridSpec(
        num_scalar_prefetch=4,
        grid=(N_KV_HEADS, len(qt)),
        in_specs=[
            pl.BlockSpec((tq, qw), lambda g, i, qt_, kt_, f_, l_: (qt_[i], g)),
            pl.BlockSpec((tk, HEAD), lambda g, i, qt_, kt_, f_, l_: (kt_[i], g)),
            pl.BlockSpec((tk, HEAD), lambda g, i, qt_, kt_, f_, l_: (kt_[i], g)),
        ],
        out_specs=pl.BlockSpec((tq, qw), lambda g, i, qt_, kt_, f_, l_: (qt_[i], g)),
        scratch_shapes=[pltpu.VMEM((KV_GROUP * tq, LANES), F32), pltpu.VMEM((KV_GROUP * tq, 2 * HEAD), F32)],
    )
    return pl.pallas_call(
        functools.partial(_attn_body, tq=tq),
        grid_spec=grid_spec,
        out_shape=jax.ShapeDtypeStruct((t, ATT_WIDTH), F32),
        compiler_params=_cparams(("parallel", "arbitrary")),
        name="attention",
    )(*tabs, qh, kh, vh)


_LEVEL_HALVES = (8, 16, 32, 64)


def _level_ref_groups(rev, hs):
    g = np.arange(CHUNK // SUBLANES)
    per = 2 * hs // SUBLANES
    return [int(x) for x in (g // per) * per + (hs // SUBLANES if rev else hs // SUBLANES - 1)]


def _hgrn_constants(rev):
    i = np.arange(CHUNK)
    cum = (i[None, :] >= i[:, None]) if rev else (i[None, :] <= i[:, None])
    wsel = (np.arange(SUBLANES * CHUNK)[:, None] // CHUNK) == (np.arange(CHUNK)[None, :] % SUBLANES)
    ii, jj = i[:, None], i[None, :]
    lvl = np.full((CHUNK, CHUNK), 5, np.int32)
    for n, hs in reversed(list(enumerate(_LEVEL_HALVES, start=1))):
        lvl[(ii // (2 * hs)) == (jj // (2 * hs))] = n
    lvl[(ii // SUBLANES) == (jj // SUBLANES)] = 0
    lvl[(jj < ii) if rev else (jj > ii)] = 5
    return (jnp.asarray(cum.astype(np.float32), BF16), jnp.asarray(wsel.astype(np.float32), BF16),
            jnp.asarray(lvl))


def _hgrn_chunk(rev, mild, q_raw, f_raw, v, lbv, cum_ref, wsel_ref, lvl, st_ref):
    groups = CHUNK // SUBLANES
    edge = 0 if rev else CHUNK - 1
    n_lvl = len(_LEVEL_HALVES)
    q = _silu(q_raw)
    f = lbv + (1.0 - lbv) * jax.nn.sigmoid(f_raw)
    k = 1.0 - f
    lf = jnp.log(f)
    hi = lf.astype(BF16)
    r1 = lf - hi.astype(F32)
    mid = r1.astype(BF16)
    lo = (r1 - mid.astype(F32)).astype(BF16)
    cs = jnp.dot(cum_ref[...], jnp.concatenate([hi, mid, lo], axis=1), preferred_element_type=F32)
    b = cs[:, :HEAD] + cs[:, HEAD:2 * HEAD] + cs[:, 2 * HEAD:]
    b3 = b.reshape(groups, SUBLANES, HEAD)
    g_first = jnp.broadcast_to(b3[:, 0:1, :], b3.shape)
    g_last = jnp.broadcast_to(b3[:, SUBLANES - 1:SUBLANES, :], b3.shape)
    near, far = (g_last, g_first) if rev else (g_first, g_last)

    level_scores = []
    for hs in _LEVEL_HALVES:
        b_ref = jnp.concatenate([far[g:g + 1] for g in _level_ref_groups(rev, hs)], axis=0)
        a = jnp.exp(-jnp.abs(b - b_ref.reshape(CHUNK, HEAD)))
        level_scores.append(lax.dot_general((q * a).astype(BF16), (k * a).astype(BF16), _NT,
                                            preferred_element_type=F32))

    def group_scores_factored():
        d = b - near.reshape(CHUNK, HEAD)
        return lax.dot_general((q * jnp.exp(d)).astype(BF16), (k * jnp.exp(-d)).astype(BF16), _NT,
                               preferred_element_type=F32)

    def group_scores_pairwise():
        sub = lax.broadcasted_iota(jnp.int32, (groups, SUBLANES, HEAD), 1)
        q3 = q.reshape(groups, SUBLANES, HEAD)
        k3 = k.reshape(groups, SUBLANES, HEAD)
        slabs = []
        for j in range(SUBLANES):
            bj = jnp.broadcast_to(b3[:, j:j + 1, :], b3.shape)
            kj = jnp.broadcast_to(k3[:, j:j + 1, :], k3.shape)
            ok = (sub <= j) if rev else (sub >= j)
            e = jnp.exp(jnp.where(ok, b3 - bj, -jnp.inf))
            slabs.append((q3 * kj * e).reshape(CHUNK, HEAD).astype(BF16))
        return jnp.dot(jnp.concatenate(slabs, axis=1), wsel_ref[...], preferred_element_type=F32)

    scores = group_scores_factored() if mild else group_scores_pairwise()
    scores = jnp.where(lvl == 0, scores, 0.0)
    for n, s in enumerate(level_scores, start=1):
        scores = jnp.where(lvl == n, s, scores)
    o_intra = jnp.dot(scores.astype(BF16), v.astype(BF16), preferred_element_type=F32)

    st = st_ref[...]
    b_edge = b[edge:edge + 1, :]
    o_inter = lax.dot_general((q * jnp.exp(b)).astype(BF16), st.astype(BF16), _NT, preferred_element_type=F32)
    k_end = (k * jnp.exp(b_edge - b)).astype(BF16)
    st_ref[...] = st * jnp.exp(b_edge) + jnp.dot(v.T.astype(BF16), k_end, preferred_element_type=F32)
    return o_intra + o_inter


def _hgrn_body(rsf_ref, rsb_ref, qf_ref, ff_ref, vf_ref, qb_ref, fb_ref, vb_ref, lbf_ref, lbb_ref,
               cumf_ref, cumb_ref, wsel_ref, lvlf_ref, lvlb_ref, of_ref, ob_ref, stf_ref, stb_ref, *, nchunk, nblk):
    it = pl.program_id(1)

    @pl.when(rsf_ref[it] == 1)
    def _():
        stf_ref[...] = jnp.zeros_like(stf_ref)

    @pl.when(rsb_ref[nblk - 1 - it] == 1)
    def _():
        stb_ref[...] = jnp.zeros_like(stb_ref)

    lbf = lbf_ref[...]
    lbb = lbb_ref[...]
    lvlf = lvlf_ref[...]
    lvlb = lvlb_ref[...]

    def run(mild):
        def chunk(ci, carry):
            rf = pl.ds(pl.multiple_of(ci * CHUNK, CHUNK), CHUNK)
            rb = pl.ds(pl.multiple_of((nchunk - 1 - ci) * CHUNK, CHUNK), CHUNK)
            of_ref[rf, :] = _hgrn_chunk(False, mild, qf_ref[rf, :], ff_ref[rf, :], vf_ref[rf, :], lbf, cumf_ref,
                                        wsel_ref, lvlf, stf_ref)
            ob_ref[rb, :] = _hgrn_chunk(True, mild, qb_ref[rb, :], fb_ref[rb, :], vb_ref[rb, :], lbb, cumb_ref,
                                        wsel_ref, lvlb, stb_ref)
            return carry
        lax.fori_loop(0, nchunk, chunk, 0, unroll=HGRN_UNROLL if mild else 1)

    mild = jnp.minimum(jnp.min(ff_ref[...]), jnp.min(fb_ref[...])) >= -(GROUP_LOG_BOUND - 1.0)

    @pl.when(mild)
    def _():
        run(True)

    @pl.when(jnp.logical_not(mild))
    def _():
        run(False)


def _hgrn(proj, lb_f, lb_b, seq_lens):
    t = proj.shape[0]
    lblk = min(512, min(seq_lens))
    nblk = t // lblk
    starts = np.cumsum([0] + list(seq_lens))
    reset_f = np.zeros((nblk,), np.int32)
    reset_b = np.zeros((nblk,), np.int32)
    for s, l in zip(starts[:-1], seq_lens):
        reset_f[s // lblk] = 1
        reset_b[(s + l) // lblk - 1] = 1
    cum_f, wsel, lvl_f = _hgrn_constants(False)
    cum_b, _, lvl_b = _hgrn_constants(True)

    def col(off, rev):
        return lambda h, i, rf, rb: (nblk - 1 - i if rev else i, off // HEAD + h)

    const = lambda h, i, rf, rb: (0, 0)
    head = lambda h, i, rf, rb: (h, 0, 0)
    grid_spec = pltpu.PrefetchScalarGridSpec(
        num_scalar_prefetch=2,
        grid=(N_HGRN_HEADS, nblk),
        in_specs=[
            pl.BlockSpec((lblk, HEAD), col(COL_QH, False)),
            pl.BlockSpec((lblk, HEAD), col(COL_FF, False)),
            pl.BlockSpec((lblk, HEAD), col(COL_IH, False)),
            pl.BlockSpec((lblk, HEAD), col(COL_QH, True)),
            pl.BlockSpec((lblk, HEAD), col(COL_FB, True)),
            pl.BlockSpec((lblk, HEAD), col(COL_IH, True)),
            pl.BlockSpec((None, 1, HEAD), head),
            pl.BlockSpec((None, 1, HEAD), head),
            pl.BlockSpec(cum_f.shape, const),
            pl.BlockSpec(cum_b.shape, const),
            pl.BlockSpec(wsel.shape, const),
            pl.BlockSpec(lvl_f.shape, const),
            pl.BlockSpec(lvl_b.shape, const),
        ],
        out_specs=[pl.BlockSpec((lblk, HEAD), lambda h, i, rf, rb: (i, h)),
                   pl.BlockSpec((lblk, HEAD), lambda h, i, rf, rb: (nblk - 1 - i, h))],
        scratch_shapes=[pltpu.VMEM((HEAD, HEAD), F32), pltpu.VMEM((HEAD, HEAD), F32)],
    )
    return pl.pallas_call(
        functools.partial(_hgrn_body, nchunk=lblk // CHUNK, nblk=nblk),
        grid_spec=grid_spec,
        out_shape=[jax.ShapeDtypeStruct((t, HGRN_WIDTH), F32), jax.ShapeDtypeStruct((t, HGRN_WIDTH), F32)],
        compiler_params=_cparams(("parallel", "arbitrary")),
        name="hgrn",
    )(jnp.asarray(reset_f), jnp.asarray(reset_b), proj, proj, proj, proj, proj, proj, lb_f, lb_b,
      cum_f, cum_b, wsel, lvl_f, lvl_b)


def _merge_outproj_body(of_ref, ob_ref, g_ref, a_ref, x_ref, hw_ref, aw_ref, w_ref, o_ref, mg_ref, *, tm, rc):
    @pl.when(pl.program_id(1) == 0)
    def _():
        def one(r):
            for h in range(N_HGRN_HEADS):
                sl = slice(h * HEAD, (h + 1) * HEAD)
                o = of_ref[r, sl] + ob_ref[r, sl]
                mg_ref[r, sl] = (_rms_rows(o, hw_ref[...]) * _silu(g_ref[r, sl])).astype(BF16)
            mg_ref[r, HGRN_WIDTH:] = _rms_rows(a_ref[r, :], aw_ref[...]).astype(BF16)
        _for_row_chunks(tm, rc, one)

    o_ref[...] = x_ref[...] + jnp.dot(mg_ref[...], w_ref[...].astype(BF16), preferred_element_type=F32)


def _merge_outproj(o_f, o_b, proj, attn, x, hgrn_norm_w, attn_norm_w, w_out, layer):
    t, d = x.shape
    tm = _row_tile(t, 1024)
    tn = 512
    return pl.pallas_call(
        functools.partial(_merge_outproj_body, tm=tm, rc=min(256, tm)),
        grid=(t // tm, d // tn),
        in_specs=[
            pl.BlockSpec((tm, HGRN_WIDTH), lambda i, j: (i, 0)),
            pl.BlockSpec((tm, HGRN_WIDTH), lambda i, j: (i, 0)),
            pl.BlockSpec((tm, HGRN_WIDTH), lambda i, j: (i, COL_GH // HGRN_WIDTH)),
            pl.BlockSpec((tm, ATT_WIDTH), lambda i, j: (i, 0)),
            pl.BlockSpec((tm, tn), lambda i, j: (i, j)),
            pl.BlockSpec((None, 1, HEAD), lambda i, j: (layer, 0, 0)),
            pl.BlockSpec((None, 1, ATT_WIDTH), lambda i, j: (layer, 0, 0)),
            pl.BlockSpec((None, d, tn), lambda i, j: (layer, 0, j)),
        ],
        out_specs=pl.BlockSpec((tm, tn), lambda i, j: (i, j)),
        out_shape=jax.ShapeDtypeStruct((t, d), F32),
        scratch_shapes=[pltpu.VMEM((tm, d), BF16)],
        compiler_params=_cparams(("parallel", "arbitrary")),
        name="merge_outproj",
    )(o_f, o_b, proj, attn, x, hgrn_norm_w.reshape(-1, 1, HEAD), attn_norm_w.reshape(-1, 1, ATT_WIDTH), w_out)


def _router_body(x_ref, nw_ref, w_ref, b_ref, o_ref, cnt_ref):
    @pl.when(pl.program_id(0) == 0)
    def _():
        cnt_ref[...] = jnp.zeros_like(cnt_ref)

    xn = _rms_rows(x_ref[...], nw_ref[...])
    w = w_ref[...]
    xh = xn.astype(BF16)
    xl = (xn - xh.astype(F32)).astype(BF16)
    wh = w.astype(BF16)
    wl = (w - wh.astype(F32)).astype(BF16)
    logits = (jnp.dot(xh, wh, preferred_element_type=F32) + jnp.dot(xl, wh, preferred_element_type=F32)
              + jnp.dot(xh, wl, preferred_element_type=F32)) + b_ref[...]
    lane = lax.broadcasted_iota(jnp.int32, logits.shape, 1).astype(F32)
    neg = -jnp.inf
    big = float(LANES)

    def first_argmax(vals):
        top = jnp.max(vals, axis=-1, keepdims=True)
        idx = jnp.min(jnp.where(vals == top, lane, big), axis=-1, keepdims=True)
        return top, idx

    gl = jnp.where(lane < N_GROUPS, logits, neg)
    gmax, gsel = first_argmax(gl)
    g_w = 1.0 / jnp.sum(jnp.exp(gl - gmax), axis=-1, keepdims=True)
    lo = N_GROUPS + EXPERTS_PER_GROUP * gsel
    el = jnp.where((lane >= lo) & (lane < lo + EXPERTS_PER_GROUP), logits, neg)
    t1, i1 = first_argmax(el)
    t2, i2 = first_argmax(jnp.where(lane == i1, neg, el))
    e = jnp.exp(t2 - t1)
    w1 = g_w / (1.0 + e)
    w2 = g_w * e / (1.0 + e)
    e1 = i1 - N_GROUPS
    e2 = i2 - N_GROUPS
    o_ref[...] = jnp.where(lane == 0, e1, jnp.where(lane == 1, e2, jnp.where(lane == 2, w1, jnp.where(lane == 3, w2, 0.0))))
    hits = jnp.where(lane == e1, 1.0, 0.0) + jnp.where(lane == e2, 1.0, 0.0)
    cnt_ref[...] += jnp.sum(hits, axis=0, keepdims=True)


def _router(h, ffn_norm_w, w_group, b_group, w_router, b_router, layer):
    t, d = h.shape
    tm = _row_tile(t, 256)
    pad = LANES - N_GROUPS - N_EXPERTS
    w = jnp.concatenate([w_group[layer], w_router[layer], jnp.zeros((d, pad), F32)], axis=1)
    b = jnp.concatenate([b_group[layer], b_router[layer], jnp.zeros((pad,), F32)])[None, :]
    return pl.pallas_call(
        _router_body,
        grid=(t // tm,),
        in_specs=[
            pl.BlockSpec((tm, d), lambda i: (i, 0)),
            pl.BlockSpec((None, 1, d), lambda i: (layer, 0, 0)),
            pl.BlockSpec((d, LANES), lambda i: (0, 0)),
            pl.BlockSpec((1, LANES), lambda i: (0, 0)),
        ],
        out_specs=[pl.BlockSpec((tm, LANES), lambda i: (i, 0)), pl.BlockSpec((1, LANES), lambda i: (0, 0))],
        out_shape=[jax.ShapeDtypeStruct((t, LANES), F32), jax.ShapeDtypeStruct((1, LANES), F32)],
        compiler_params=_cparams(("arbitrary",)),
        name="router",
    )(h, ffn_norm_w.reshape(-1, 1, d), w, b)


def _dispatch_tables(route, counts, t):
    tm = MOE_TILE
    a = t * TOP_K
    n_tiles = a // tm + N_EXPERTS
    eid = route[:, :TOP_K].astype(jnp.int32).reshape(a)
    wts = route[:, TOP_K:2 * TOP_K].reshape(a)
    order = jnp.argsort(eid, stable=True).astype(jnp.int32)
    counts = counts[0, :N_EXPERTS].astype(jnp.int32)
    seg_start = jnp.cumsum(counts) - counts
    padded = (counts + tm - 1) // tm * tm
    pad_end = jnp.cumsum(padded)
    pad_start = pad_end - padded
    n_valid = (pad_end[-1] // tm).astype(jnp.int32)
    tile = jnp.arange(n_tiles, dtype=jnp.int32)
    tile_expert = jnp.sum((pad_end[None, :] <= (tile * tm)[:, None]).astype(jnp.int32), axis=1)
    tile_expert = jnp.minimum(tile_expert, N_EXPERTS - 1)
    tile_expert = jnp.where(tile < n_valid, tile_expert, tile_expert[jnp.maximum(n_valid - 1, 0)])
    first_row = tile * tm - pad_start[tile_expert]
    n_rows = jnp.where(tile < n_valid, jnp.clip(counts[tile_expert] - first_row, 0, tm), 0).astype(jnp.int32)
    within = jnp.arange(tm, dtype=jnp.int32)[None, :]
    valid = within < n_rows[:, None]
    src = order[jnp.clip((seg_start[tile_expert] + first_row)[:, None] + within, 0, a - 1)]
    row_tok = jnp.where(valid, src // TOP_K, 0).reshape(n_tiles, 1, tm)
    row_w = jnp.where(valid, wts[src], 0.0).reshape(n_tiles * tm, 1)
    row_dst = jnp.where(valid, (src % TOP_K) * t + src // TOP_K, TOP_K * t + within).reshape(n_tiles, 1, tm)
    return tile_expert, n_valid.reshape(1), row_tok, row_w, row_dst


def _cast_rows(src_ref, dst_ref, rc):
    def one(r):
        dst_ref[r, :] = src_ref[r, :].astype(dst_ref.dtype)
    _for_row_chunks(src_ref.shape[0], rc, one)


def _expert_changed(te_ref, i):
    return (i == 0) | (te_ref[i] != te_ref[jnp.maximum(i - 1, 0)])


def _gateup_body(te_ref, nv_ref, rt_ref, rtn_ref, x_hbm, nw_ref, wg_ref, wu_ref, o_ref, xbuf, xn_ref, wgb, wub, sem,
                 *, tm):
    i = pl.program_id(0)
    nv = nv_ref[0]
    slot = i % 2

    def row_copy(idx_ref, sl, r):
        return pltpu.make_async_copy(x_hbm.at[pl.ds(idx_ref[0, r], 1), :], xbuf.at[sl, pl.ds(r, 1), :], sem.at[sl])

    def wait_slot(sl):
        pltpu.make_async_copy(x_hbm.at[pl.ds(0, tm), :], xbuf.at[sl], sem.at[sl]).wait()

    @pl.when((i == 0) & (nv > 0))
    def _():
        def body(r, carry):
            row_copy(rt_ref, 0, r).start()
            return carry
        lax.fori_loop(0, tm, body, 0, unroll=DMA_ISSUE_UNROLL)

    @pl.when(i < nv)
    def _():
        wait_slot(slot)

        @pl.when(_expert_changed(te_ref, i))
        def _():
            _cast_rows(wg_ref, wgb, 256)
            _cast_rows(wu_ref, wub, 256)

        xn_ref[...] = _rms_rows(xbuf[slot], nw_ref[...]).astype(BF16)
        de = o_ref.shape[1]
        nblk = de // MOE_COL_BLOCK
        per = tm // nblk
        for j in range(nblk):
            for r in range(j * per, (j + 1) * per):
                row_copy(rtn_ref, 1 - slot, r).start()
            cols = slice(j * MOE_COL_BLOCK, (j + 1) * MOE_COL_BLOCK)
            g = jnp.dot(xn_ref[...], wgb[:, cols], preferred_element_type=F32)
            u = jnp.dot(xn_ref[...], wub[:, cols], preferred_element_type=F32)
            o_ref[:, cols] = (_silu(g) * u).astype(BF16)

        @pl.when(i == nv - 1)
        def _():
            wait_slot(1 - slot)

    @pl.when(i >= nv)
    def _():
        o_ref[...] = jnp.zeros_like(o_ref)


def _expert_gateup(h, ffn_norm_w, w_gate, w_up, tile_expert, n_valid, row_tok, layer):
    t, d = h.shape
    tm = MOE_TILE
    n_tiles = row_tok.shape[0]
    de = w_gate.shape[-1]
    smem_rows = lambda f: pl.BlockSpec((None, 1, tm), f, memory_space=pltpu.SMEM)
    wspec = pl.BlockSpec((None, None, d, de), lambda i, te, nv: (layer, te[i], 0, 0))
    grid_spec = pltpu.PrefetchScalarGridSpec(
        num_scalar_prefetch=2,
        grid=(n_tiles,),
        in_specs=[
            smem_rows(lambda i, te, nv: (i, 0, 0)),
            smem_rows(lambda i, te, nv: (jnp.minimum(i + 1, n_tiles - 1), 0, 0)),
            pl.BlockSpec(memory_space=pl.ANY),
            pl.BlockSpec((None, 1, d), lambda i, te, nv: (layer, 0, 0)),
            wspec,
            wspec,
        ],
        out_specs=pl.BlockSpec((tm, de), lambda i, te, nv: (i, 0)),
        scratch_shapes=[pltpu.VMEM((2, tm, d), F32), pltpu.VMEM((tm, d), BF16), pltpu.VMEM((d, de), BF16),
                        pltpu.VMEM((d, de), BF16), pltpu.SemaphoreType.DMA((2,))],
    )
    return pl.pallas_call(
        functools.partial(_gateup_body, tm=tm),
        grid_spec=grid_spec,
        out_shape=jax.ShapeDtypeStruct((n_tiles * tm, de), BF16),
        compiler_params=_cparams(("arbitrary",)),
        name="expert_gateup",
    )(tile_expert, n_valid, row_tok, row_tok, h, ffn_norm_w.reshape(-1, 1, d), w_gate, w_up)


def _down_body(te_ref, nv_ref, rd_ref, rdp_ref, h_ref, rw_ref, wd_ref, y_hbm, ybuf, wdb, sem, *, tm, spare_row0):
    i = pl.program_id(0)
    nv = nv_ref[0]
    slot = i % 2

    def row_copy(idx_ref, sl, r):
        return pltpu.make_async_copy(ybuf.at[sl, pl.ds(r, 1), :], y_hbm.at[pl.ds(idx_ref[0, r], 1), :], sem.at[sl])

    def wait_slot(sl):
        pltpu.make_async_copy(ybuf.at[sl], y_hbm.at[pl.ds(0, tm), :], sem.at[sl]).wait()

    @pl.when(i == 0)
    def _():
        ybuf[...] = jnp.zeros_like(ybuf)
        init = pltpu.make_async_copy(ybuf.at[0], y_hbm.at[pl.ds(spare_row0, tm), :], sem.at[0])
        init.start()
        init.wait()

    @pl.when(i < nv)
    def _():
        @pl.when(i >= 1)
        def _():
            wait_slot(slot)

        @pl.when(_expert_changed(te_ref, i))
        def _():
            _cast_rows(wd_ref, wdb, 256)

        d = ybuf.shape[2]
        nblk = d // MOE_COL_BLOCK
        per = tm // nblk
        for j in range(nblk):
            for r in range(j * per, (j + 1) * per):
                row_copy(rdp_ref, 1 - slot, r).start()
            cols = slice(j * MOE_COL_BLOCK, (j + 1) * MOE_COL_BLOCK)
            ybuf[slot, :, cols] = jnp.dot(h_ref[...], wdb[:, cols], preferred_element_type=F32) * rw_ref[...]

        @pl.when(i == nv - 1)
        def _():
            wait_slot(1 - slot)

            def body(r, carry):
                row_copy(rd_ref, slot, r).start()
                return carry
            lax.fori_loop(0, tm, body, 0, unroll=DMA_ISSUE_UNROLL)
            wait_slot(slot)


def _expert_down(hid, w_down, tile_expert, n_valid, row_dst, row_w, t, layer):
    tm = MOE_TILE
    n_tiles = row_dst.shape[0]
    de, d = w_down.shape[-2:]
    spare = TOP_K * t + jnp.arange(tm, dtype=jnp.int32).reshape(1, 1, tm)
    row_dst_prev = jnp.concatenate([spare, row_dst[:-1]], axis=0)
    smem_rows = pl.BlockSpec((None, 1, tm), lambda i, te, nv: (i, 0, 0), memory_space=pltpu.SMEM)
    grid_spec = pltpu.PrefetchScalarGridSpec(
        num_scalar_prefetch=2,
        grid=(n_tiles,),
        in_specs=[
            smem_rows,
            smem_rows,
            pl.BlockSpec((tm, de), lambda i, te, nv: (i, 0)),
            pl.BlockSpec((tm, 1), lambda i, te, nv: (i, 0)),
            pl.BlockSpec((None, None, de, d), lambda i, te, nv: (layer, te[i], 0, 0)),
        ],
        out_specs=pl.BlockSpec(memory_space=pl.ANY),
        scratch_shapes=[pltpu.VMEM((2, tm, d), F32), pltpu.VMEM((de, d), BF16), pltpu.SemaphoreType.DMA((2,))],
    )
    return pl.pallas_call(
        functools.partial(_down_body, tm=tm, spare_row0=TOP_K * t),
        grid_spec=grid_spec,
        out_shape=jax.ShapeDtypeStruct((TOP_K * t + tm, d), F32),
        compiler_params=_cparams(("arbitrary",)),
        name="expert_down",
    )(tile_expert, n_valid, row_dst, row_dst_prev, hid, row_w, w_down)


def _combine_body(h_ref, y0_ref, y1_ref, nw_ref, o_ref, *, final):
    s = h_ref[...] + (y0_ref[...] + y1_ref[...])
    o_ref[...] = _rms_rows(s, nw_ref[...]) if final else s


def _combine(h, y, final_norm_w, final):
    t, d = h.shape
    tm = _row_tile(t, 256)
    nb = t // tm
    return pl.pallas_call(
        functools.partial(_combine_body, final=final),
        grid=(nb,),
        in_specs=[
            pl.BlockSpec((tm, d), lambda i: (i, 0)),
            pl.BlockSpec((tm, d), lambda i: (i, 0)),
            pl.BlockSpec((tm, d), lambda i: (i + nb, 0)),
            pl.BlockSpec((1, d), lambda i: (0, 0)),
        ],
        out_specs=pl.BlockSpec((tm, d), lambda i: (i, 0)),
        out_shape=jax.ShapeDtypeStruct((t, d), F32),
        compiler_params=_cparams(("parallel",)),
        name="combine_final" if final else "combine",
    )(h, y, y, final_norm_w.reshape(1, d))


def _hgrn_lower_bounds(lb_param):
    p = jax.nn.softmax(lb_param.astype(F32), axis=1)
    return jnp.cumsum(p, axis=1) - p[:, :1]


def _trunk(x, seq_lens, mix_norm_w, w_in, hgrn_lb, hgrn_out_norm_w, q_norm_w, k_norm_w, attn_out_norm_w, w_out,
           ffn_norm_w, w_group, b_group, w_router, b_router, w_gate_e, w_up_e, w_down_e, final_norm_w):
    t = x.shape[0]
    depth = w_in.shape[0]
    lb = _hgrn_lower_bounds(hgrn_lb).reshape(2, depth, N_HGRN_HEADS, 1, HEAD)
    cos, sin = _rope_tables(max(seq_lens))
    h = x
    for l in range(depth):
        proj = _norm_inproj(h, mix_norm_w, w_in, l)
        qh, kh, vh = _qkprep(proj, cos, sin, q_norm_w, k_norm_w, l, seq_lens)
        attn = _attention(qh, kh, vh, seq_lens)
        o_f, o_b = _hgrn(proj, lb[0, l], lb[1, l], seq_lens)
        h = _merge_outproj(o_f, o_b, proj, attn, h, hgrn_out_norm_w, attn_out_norm_w, w_out, l)
        route, counts = _router(h, ffn_norm_w, w_group, b_group, w_router, b_router, l)
        tile_expert, n_valid, row_tok, row_w, row_dst = _dispatch_tables(route, counts, t)
        hid = _expert_gateup(h, ffn_norm_w, w_gate_e, w_up_e, tile_expert, n_valid, row_tok, l)
        y = _expert_down(hid, w_down_e, tile_expert, n_valid, row_dst, row_w, t, l)
        h = _combine(h, y, final_norm_w, final=(l == depth - 1))
    return h


def kernel(x_prompt, x_sample, mix_norm_w, w_in, hgrn_lb, hgrn_out_norm_w, q_norm_w, k_norm_w, attn_out_norm_w, w_out,
           ffn_norm_w, w_group, b_group, w_router, b_router, w_gate_e, w_up_e, w_down_e, final_norm_w):
    bp, lp, d = x_prompt.shape
    bs, ls, _ = x_sample.shape
    seq_lens = (lp,) * bp + (ls,) * bs
    x = jnp.concatenate([x_prompt.reshape(bp * lp, d), x_sample.reshape(bs * ls, d)], axis=0)
    y = _trunk(x, seq_lens, mix_norm_w, w_in, hgrn_lb, hgrn_out_norm_w, q_norm_w, k_norm_w, attn_out_norm_w, w_out,
               ffn_norm_w, w_group, b_group, w_router, b_router, w_gate_e, w_up_e, w_down_e, final_norm_w)
    return y[:bp * lp].reshape(bp, lp, d), y[bp * lp:].reshape(bs, ls, d)
```
